```python
import math
import jax, jax.numpy as jnp
from jax import lax
import numpy as np


D_MODEL = 4096
BATCH = 4
SEQ = 2048
DEPTH = 4
DEC_BATCH = 8
DEC_SEQ = 8
PAST_LEN = 8192
PAGE_SIZE = 128

CONV_W = 3
A_WIDTH = D_MODEL // 2
GLA_HEADS = 4
GLA_VDIM = D_MODEL // 2
GLA_KDIM = D_MODEL // 4
GLA_DK = GLA_KDIM // GLA_HEADS
GLA_DV = GLA_VDIM // GLA_HEADS
GLA_GATE_RANK = 16
GLA_TAU = 16.0
GLA_CHUNK = 64
FOX_DH = 128
FOX_HEADS = D_MODEL // FOX_DH
Q_BLOCK = 128
D_FF = 256 * ((8 * D_MODEL // 3 + 255) // 256)
EPS = 1e-6
N_EVEN = (DEPTH + 1) // 2
N_ODD = DEPTH // 2
EVEN_WIDTHS = (A_WIDTH, A_WIDTH, A_WIDTH, GLA_KDIM, GLA_KDIM, GLA_VDIM, GLA_VDIM, GLA_GATE_RANK)
ODD_WIDTHS = (D_MODEL, D_MODEL, D_MODEL, FOX_HEADS)

kernel_name = 'hybrid_conv_gla_fox_decode_step'


def _split(x, widths):
    out, idx = [], 0
    for w in widths:
        out.append(x[..., idx:idx + w])
        idx += w
    return out


def rms_norm(x, g):
    xf = x.astype(jnp.float32)
    y = xf * lax.rsqrt(jnp.mean(xf * xf, axis=-1, keepdims=True) + EPS)
    return (y * g.astype(jnp.float32)).astype(x.dtype)


def causal_dwconv(u, hist, w):
    L = u.shape[1]
    up = jnp.concatenate([hist.astype(u.dtype), u], axis=1)
    y = w[0] * up[:, 0:L]
    for i in range(1, CONV_W):
        y = y + w[i] * up[:, i:i + L]
    return y, up[:, L:]


def gla_chunked(q, k, v, log_a, s0):
    B, L, H, DK = q.shape
    DV = v.shape[-1]
    C = math.gcd(L, GLA_CHUNK)
    n = L // C
    f32 = jnp.float32

    def blk(t):
        return t.astype(f32).reshape(B, n, C, H, t.shape[-1]).transpose(1, 0, 3, 2, 4)

    qc, kc, vc, ac = blk(q), blk(k), blk(v), blk(log_a)
    b = jnp.cumsum(ac, axis=3)
    b_last = b[:, :, :, -1:, :]
    q_dec = qc * jnp.exp(b)
    k_inv = kc * jnp.exp(-b)
    k_tail = kc * jnp.exp(b_last - b)
    causal = jnp.tril(jnp.ones((C, C), dtype=bool))
    att = jnp.where(causal, jnp.einsum('nbhtd,nbhsd->nbhts', q_dec, k_inv), 0.0)
    o_intra = jnp.einsum('nbhts,nbhse->nbhte', att, vc)

    def step(S, xs):
        qd, kt, vv, bl = xs
        o = jnp.einsum('bhtd,bhde->bhte', qd, S)
        S = jnp.exp(bl)[..., 0, :, None] * S + jnp.einsum('bhsd,bhse->bhde', kt, vv)
        return S, o

    s_fin, o_inter = lax.scan(step, s0.astype(f32), (q_dec, k_tail, vc, b_last))
    o = (o_inter + o_intra).transpose(1, 0, 3, 2, 4).reshape(B, L, H, DV)
    return o, s_fin


def even_mixer(h, conv_hist, s0, w_in, w_conv, w_gate_up, b_gate, g_norm, w_out):
    B, L, _ = h.shape
    proj = h @ w_in
    a_b, a_c, a_h, q, k, v, r, g_low = _split(proj, EVEN_WIDTHS)
    z, conv_state = causal_dwconv(a_c * a_h, conv_hist, w_conv)
    y_a = a_b * z
    log_a = jax.nn.log_sigmoid((g_low @ w_gate_up + b_gate).astype(jnp.float32)) / GLA_TAU
    o, s_new = gla_chunked(q.reshape(B, L, GLA_HEADS, GLA_DK) * (GLA_DK ** -0.5),
                           k.reshape(B, L, GLA_HEADS, GLA_DK),
                           v.reshape(B, L, GLA_HEADS, GLA_DV),
                           log_a.reshape(B, L, GLA_HEADS, GLA_DK), s0)
    o = rms_norm(o, g_norm.reshape(GLA_HEADS, GLA_DV))
    y_b = o.reshape(B, L, GLA_VDIM).astype(h.dtype) * jax.nn.silu(r)
    y = jnp.concatenate([y_a, y_b], axis=-1) @ w_out
    return y, conv_state, s_new.astype(s0.dtype)


def fox_qkv(h, w_in, b_f):
    B, L, _ = h.shape
    q, k, v, fl = _split(h @ w_in, ODD_WIDTHS)
    logf = jax.nn.log_sigmoid((fl + b_f).astype(jnp.float32))
    shp = (B, L, FOX_HEADS, FOX_DH)
    return q.reshape(shp), k.reshape(shp), v.reshape(shp), logf


def fox_attention(q, k, v, logf_k, q_start):
    B, Lq, H, Dh = q.shape
    Lk = k.shape[1]
    r = lax.cumsum(logf_k, axis=1, reverse=True) - logf_k
    rq = r[:, q_start:q_start + Lq]
    qb = math.gcd(Lq, Q_BLOCK)
    nb = Lq // qb
    q_blocks = q.reshape(B, nb, qb, H, Dh).transpose(1, 0, 2, 3, 4)
    r_blocks = rq.reshape(B, nb, qb, H).transpose(1, 0, 3, 2)
    r_keys = r.transpose(0, 2, 1)
    k_pos = jnp.arange(Lk)
    scale = Dh ** -0.5

    def block(args):
        i, qi, ri = args
        s = jnp.einsum('bqhd,bkhd->bhqk', qi, k).astype(jnp.float32) * scale
        s = s + (r_keys[:, :, None, :] - ri[..., None])
        q_pos = q_start + i * qb + jnp.arange(qb)
        s = jnp.where(k_pos[None, :] <= q_pos[:, None], s, -jnp.inf)
        p = jax.nn.softmax(s, axis=-1)
        return jnp.einsum('bhqk,bkhd->bqhd', p.astype(v.dtype), v)

    out = lax.map(block, (jnp.arange(nb), q_blocks, r_blocks))
    return out.transpose(1, 0, 2, 3, 4).reshape(B, Lq, H * Dh)


def conv_ffn(h, hist, w_up, w_conv, b_conv, w_down):
    u = h @ w_up
    z, new_hist = causal_dwconv(u, hist, w_conv)
    z = z + b_conv
    gate, val = z[..., :D_FF], z[..., D_FF:]
    return (jax.nn.silu(gate) * val) @ w_down, new_hist


def setup_inputs(seed: int = 0) -> dict:
    key = jax.random.key(seed)
    ks = jax.random.split(key, 32)
    f32 = jnp.float32
    n_pages = PAST_LEN // PAGE_SIZE
    n_used = DEC_BATCH * n_pages
    n_pool = n_used + n_used // 4
    even_cols = sum(EVEN_WIDTHS)
    odd_cols = sum(ODD_WIDTHS)

    def nrm(k, shape, s):
        return jax.random.normal(k, shape, f32) * s

    b_forget = 1.0 + 4.0 * jax.random.uniform(ks[20], (N_ODD, FOX_HEADS), f32)
    page_table = jax.random.permutation(ks[8], n_pool)[:n_used].reshape(DEC_BATCH, n_pages).astype(jnp.int32)
    cache_logf = jax.nn.log_sigmoid(b_forget[:, None, None, :] + nrm(ks[6], (N_ODD, n_pool, PAGE_SIZE, FOX_HEADS), 1.0))
    return {
        'x_prompt': nrm(ks[0], (BATCH, SEQ, D_MODEL), 1.0),
        'x_sample': nrm(ks[1], (DEC_BATCH, DEC_SEQ, D_MODEL), 1.0),
        'state_conv_a': nrm(ks[2], (N_EVEN, DEC_BATCH, CONV_W - 1, A_WIDTH), 1.0),
        'state_gla': nrm(ks[3], (N_EVEN, DEC_BATCH, GLA_HEADS, GLA_DK, GLA_DV), 0.5),
        'cache_k': nrm(ks[4], (N_ODD, n_pool, PAGE_SIZE, FOX_HEADS, FOX_DH), 1.0),
        'cache_v': nrm(ks[5], (N_ODD, n_pool, PAGE_SIZE, FOX_HEADS, FOX_DH), 1.0),
        'cache_logf': cache_logf,
        'state_ffn_conv': nrm(ks[7], (DEPTH, DEC_BATCH, CONV_W - 1, 2 * D_FF), 1.0),
        'page_table': page_table,
        'g_mix_pre': 1.0 + nrm(ks[9], (DEPTH, D_MODEL), 0.05),
        'g_mix_post': 1.0 + nrm(ks[10], (DEPTH, D_MODEL), 0.05),
        'g_ffn_pre': 1.0 + nrm(ks[11], (DEPTH, D_MODEL), 0.05),
        'g_ffn_post': 1.0 + nrm(ks[12], (DEPTH, D_MODEL), 0.05),
        'w_in_even': nrm(ks[13], (N_EVEN, D_MODEL, even_cols), D_MODEL ** -0.5),
        'w_conv_a': nrm(ks[14], (N_EVEN, CONV_W, A_WIDTH), CONV_W ** -0.5),
        'w_gate_up': nrm(ks[15], (N_EVEN, GLA_GATE_RANK, GLA_KDIM), GLA_GATE_RANK ** -0.5),
        'b_gate': nrm(ks[16], (N_EVEN, GLA_KDIM), 0.01),
        'g_gla_norm': 1.0 + nrm(ks[17], (N_EVEN, GLA_VDIM), 0.05),
        'w_out_even': nrm(ks[18], (N_EVEN, A_WIDTH + GLA_VDIM, D_MODEL), (A_WIDTH + GLA_VDIM) ** -0.5),
        'w_in_odd': nrm(ks[19], (N_ODD, D_MODEL, odd_cols), D_MODEL ** -0.5),
        'b_forget': b_forget,
        'w_out_odd': nrm(ks[21], (N_ODD, D_MODEL, D_MODEL), D_MODEL ** -0.5),
        'w_up': nrm(ks[22], (DEPTH, D_MODEL, 2 * D_FF), D_MODEL ** -0.5),
        'w_ffn_conv': nrm(ks[23], (DEPTH, CONV_W, 2 * D_FF), CONV_W ** -0.5),
        'b_ffn_conv': nrm(ks[24], (DEPTH, 2 * D_FF), 0.01),
        'w_down': nrm(ks[25], (DEPTH, D_FF, D_MODEL), D_FF ** -0.5),
    }


def reference(x_prompt, x_sample, state_conv_a, state_gla, cache_k, cache_v, cache_logf, state_ffn_conv,
              page_table, g_mix_pre, g_mix_post, g_ffn_pre, g_ffn_post,
              w_in_even, w_conv_a, w_gate_up, b_gate, g_gla_norm, w_out_even,
              w_in_odd, b_forget, w_out_odd, w_up, w_ffn_conv, b_ffn_conv, w_down):
    bp = x_prompt.shape[0]
    bs = x_sample.shape[0]
    past_len = page_table.shape[1] * cache_k.shape[2]
    dt = x_prompt.dtype
    conv_a_zero = jnp.zeros((bp, CONV_W - 1, A_WIDTH), dt)
    gla_zero = jnp.zeros((bp, GLA_HEADS, GLA_DK, GLA_DV), dt)
    ffn_zero = jnp.zeros((bp, CONV_W - 1, 2 * D_FF), dt)

    xp, xs = x_prompt, x_sample
    ca_p, ca_s, sg_p, sg_s = [], [], [], []
    kp_l, vp_l, lp_l, ks_l, vs_l, ls_l = [], [], [], [], [], []
    fc_p, fc_s = [], []
    for l in range(DEPTH):
        hp = rms_norm(xp, g_mix_pre[l])
        hs = rms_norm(xs, g_mix_pre[l])
        if l % 2 == 0:
            e = l // 2
            prm = (w_in_even[e], w_conv_a[e], w_gate_up[e], b_gate[e], g_gla_norm[e], w_out_even[e])
            mp, c_p, s_p = even_mixer(hp, conv_a_zero, gla_zero, *prm)
            ms, c_s, s_s = even_mixer(hs, state_conv_a[e], state_gla[e], *prm)
            ca_p.append(c_p)
            ca_s.append(c_s)
            sg_p.append(s_p)
            sg_s.append(s_s)
        else:
            o = l // 2
            qp, kp, vp, lfp = fox_qkv(hp, w_in_odd[o], b_forget[o])
            mp = fox_attention(qp, kp, vp, lfp, 0) @ w_out_odd[o]
            qs, kn, vn, lfs = fox_qkv(hs, w_in_odd[o], b_forget[o])
            k_past = cache_k[o, page_table].reshape(bs, past_len, FOX_HEADS, FOX_DH)
            v_past = cache_v[o, page_table].reshape(bs, past_len, FOX_HEADS, FOX_DH)
            f_past = cache_logf[o, page_table].reshape(bs, past_len, FOX_HEADS)
            k_all = jnp.concatenate([k_past, kn.astype(k_past.dtype)], axis=1)
            v_all = jnp.concatenate([v_past, vn.astype(v_past.dtype)], axis=1)
            f_all = jnp.concatenate([f_past.astype(jnp.float32), lfs], axis=1)
            ms = fox_attention(qs, k_all, v_all, f_all, past_len) @ w_out_odd[o]
            kp_l.append(kp)
            vp_l.append(vp)
            lp_l.append(lfp.astype(cache_logf.dtype))
            ks_l.append(kn)
            vs_l.append(vn)
            ls_l.append(lfs.astype(cache_logf.dtype))
        xp = xp + rms_norm(mp, g_mix_post[l])
        xs = xs + rms_norm(ms, g_mix_post[l])
        fprm = (w_up[l], w_ffn_conv[l], b_ffn_conv[l], w_down[l])
        fo_p, h_p = conv_ffn(rms_norm(xp, g_ffn_pre[l]), ffn_zero, *fprm)
        fo_s, h_s = conv_ffn(rms_norm(xs, g_ffn_pre[l]), state_ffn_conv[l], *fprm)
        xp = xp + rms_norm(fo_p, g_ffn_post[l])
        xs = xs + rms_norm(fo_s, g_ffn_post[l])
        fc_p.append(h_p)
        fc_s.append(h_s)

    conv_a_prompt = jnp.stack(ca_p)
    conv_a_sample = jnp.stack(ca_s)
    gla_prompt = jnp.stack(sg_p)
    gla_sample = jnp.stack(sg_s)
    k_prompt = jnp.stack(kp_l)
    v_prompt = jnp.stack(vp_l)
    logf_prompt = jnp.stack(lp_l)
    k_sample = jnp.stack(ks_l)
    v_sample = jnp.stack(vs_l)
    logf_sample = jnp.stack(ls_l)
    ffn_conv_prompt = jnp.stack(fc_p)
    ffn_conv_sample = jnp.stack(fc_s)
    return (xp, xs, conv_a_prompt, conv_a_sample, gla_prompt, gla_sample,
            k_prompt, v_prompt, logf_prompt, k_sample, v_sample, logf_sample,
            ffn_conv_prompt, ffn_conv_sample)
```

```python
import functools

import jax
import jax.numpy as jnp
from jax import lax
from jax.experimental import pallas as pl
from jax.experimental.pallas import tpu as pltpu

f32 = jnp.float32
bf16 = jnp.bfloat16

EPS = 1e-6
GLA_TAU = 16.0
GLA_CHUNK = 64
HIGHEST = lax.Precision.HIGHEST

V7X_LANES = 128
V7X_SUBLANES = 8
V7X_VMEM_BYTES = 64 * 1024 * 1024
VMEM_CAP = V7X_VMEM_BYTES - 6 * 1024 * 1024


def _pick(n, target, align):
    t = (min(target, n) // align) * align
    while t >= align:
        if n % t == 0:
            return t
        t -= align
    return n


def _params(n_axes, vmem_estimate):
    limit = min(VMEM_CAP, max(32 * 1024 * 1024, int(vmem_estimate * 1.25)))
    return pltpu.CompilerParams(dimension_semantics=("arbitrary",) * n_axes,
                                vmem_limit_bytes=limit)


def _nbytes(shape, dtype):
    n = 1
    for s in shape:
        n *= s
    return n * jnp.dtype(dtype).itemsize


def _log_sigmoid(x):
    return jnp.minimum(x, 0.0) - jnp.log1p(jnp.exp(-jnp.abs(x)))


def _rms_kernel(x_ref, g_ref, o_ref):
    x = x_ref[...]
    y = x * lax.rsqrt(jnp.mean(x * x, axis=-1, keepdims=True) + EPS)
    o_ref[...] = (y * g_ref[...]).astype(o_ref.dtype)


def _rms_cast(x2, g, name):
    m, d = x2.shape
    tm = _pick(m, 256, 16)
    est = 2 * (_nbytes((tm, d), f32) + _nbytes((tm, d), bf16)) + 4 * _nbytes((tm, d), f32)
    return pl.pallas_call(
        _rms_kernel,
        grid=(m // tm,),
        in_specs=[pl.BlockSpec((tm, d), lambda i: (i, 0)),
                  pl.BlockSpec((1, d), lambda i: (0, 0))],
        out_specs=pl.BlockSpec((tm, d), lambda i: (i, 0)),
        out_shape=jax.ShapeDtypeStruct((m, d), bf16),
        compiler_params=_params(1, est),
        name=name,
    )(x2, g.reshape(1, d))


def _post_kernel(m_ref, g_ref, x_ref, o_ref):
    m = m_ref[...]
    y = m * lax.rsqrt(jnp.mean(m * m, axis=-1, keepdims=True) + EPS)
    o_ref[...] = x_ref[...] + y * g_ref[...]


def _post_residual(m2, g, x2, name):
    m, d = x2.shape
    tm = _pick(m, 256, 8)
    est = 2 * 3 * _nbytes((tm, d), f32) + 4 * _nbytes((tm, d), f32)
    return pl.pallas_call(
        _post_kernel,
        grid=(m // tm,),
        in_specs=[pl.BlockSpec((tm, d), lambda i: (i, 0)),
                  pl.BlockSpec((1, d), lambda i: (0, 0)),
                  pl.BlockSpec((tm, d), lambda i: (i, 0))],
        out_specs=pl.BlockSpec((tm, d), lambda i: (i, 0)),
        out_shape=jax.ShapeDtypeStruct((m, d), f32),
        compiler_params=_params(1, est),
        name=name,
    )(m2, g.reshape(1, d), x2)


def _post_pre_kernel(m_ref, gpost_ref, x_ref, gpre_ref, o_ref, h_ref):
    m = m_ref[...]
    y = m * lax.rsqrt(jnp.mean(m * m, axis=-1, keepdims=True) + EPS)
    xn = x_ref[...] + y * gpost_ref[...]
    o_ref[...] = xn
    hn = xn * lax.rsqrt(jnp.mean(xn * xn, axis=-1, keepdims=True) + EPS)
    h_ref[...] = (hn * gpre_ref[...]).astype(h_ref.dtype)


def _post_pre(m2, g_post, x2, g_pre, name):
    m, d = x2.shape
    tm = _pick(m, 256, 16)
    est = 2 * (3 * _nbytes((tm, d), f32) + _nbytes((tm, d), bf16)) + 5 * _nbytes((tm, d), f32)
    row = pl.BlockSpec((tm, d), lambda i: (i, 0))
    gain = pl.BlockSpec((1, d), lambda i: (0, 0))
    return pl.pallas_call(
        _post_pre_kernel,
        grid=(m // tm,),
        in_specs=[row, gain, row, gain],
        out_specs=[row, row],
        out_shape=[jax.ShapeDtypeStruct((m, d), f32), jax.ShapeDtypeStruct((m, d), bf16)],
        compiler_params=_params(1, est),
        name=name,
    )(m2, g_post.reshape(1, d), x2, g_pre.reshape(1, d))


def _mm_kernel(x_ref, w_ref, o_ref):
    o_ref[...] = lax.dot_general(
        x_ref[...], w_ref[...], (((1,), (0,)), ((), ())),
        preferred_element_type=f32).astype(o_ref.dtype)


def _mm(x, w3, layer, col0, n, *, tm_target, tn_target, w_single_buffer=False,
        out_dtype=f32, name):
    m, k = x.shape
    assert w3.shape[1] == k
    tm = _pick(m, tm_target, 16)
    tn = _pick(n, tn_target, V7X_LANES)
    assert col0 % tn == 0 and n % tn == 0 and m % tm == 0
    cb = col0 // tn
    w_bufs = 1 if w_single_buffer else 2
    est = (2 * _nbytes((tm, k), x.dtype) + w_bufs * _nbytes((k, tn), w3.dtype)
           + _nbytes((k, tn), bf16) + 2 * _nbytes((tm, tn), out_dtype) + _nbytes((tm, tn), f32))
    w_kwargs = dict(pipeline_mode=pl.Buffered(1)) if w_single_buffer else {}
    return pl.pallas_call(
        _mm_kernel,
        grid=(n // tn, m // tm),
        in_specs=[pl.BlockSpec((tm, k), lambda j, i: (i, 0)),
                  pl.BlockSpec((None, k, tn), lambda j, i: (layer, 0, cb + j), **w_kwargs)],
        out_specs=pl.BlockSpec((tm, tn), lambda j, i: (i, j)),
        out_shape=jax.ShapeDtypeStruct((m, n), out_dtype),
        compiler_params=_params(2, est),
        name=name,
    )(x, w3)


def _mm_tail(x, w_tail, name):
    m, k = x.shape
    n = w_tail.shape[1]
    tm = _pick(m, 1024, 16)
    est = 2 * _nbytes((tm, k), x.dtype) + 2 * _nbytes((k, V7X_LANES), f32) + 4 * _nbytes((tm, V7X_LANES), f32)
    return pl.pallas_call(
        _mm_kernel,
        grid=(m // tm,),
        in_specs=[pl.BlockSpec((tm, k), lambda i: (i, 0)),
                  pl.BlockSpec((k, n), lambda i: (0, 0))],
        out_specs=pl.BlockSpec((tm, n), lambda i: (i, 0)),
        out_shape=jax.ShapeDtypeStruct((m, n), f32),
        compiler_params=_params(1, est),
        name=name,
    )(x, w_tail)


def _conv3(u, h0, h1, w_ref):
    row = lax.broadcasted_iota(jnp.int32, u.shape, 0)
    u1 = jnp.where(row == 0, h1, pltpu.roll(u, 1, 0))
    u2 = jnp.where(row == 0, h0, jnp.where(row == 1, h1, pltpu.roll(u, 2, 0)))
    return w_ref[0:1, :] * u2 + w_ref[1:2, :] * u1 + w_ref[2:3, :] * u


def _conva_kernel(*refs, has_hist):
    if has_hist:
        ab_ref, ac_ref, ah_ref, w_ref, hist_ref, y_ref, st_ref = refs
        h0, h1 = hist_ref[0, 0:1, :], hist_ref[0, 1:2, :]
    else:
        ab_ref, ac_ref, ah_ref, w_ref, y_ref, st_ref = refs
        h0 = h1 = jnp.zeros((1, ac_ref.shape[2]), f32)
    p = ac_ref[0] * ah_ref[0]
    z = _conv3(p, h0, h1, w_ref)
    y_ref[0] = (ab_ref[0] * z).astype(y_ref.dtype)
    length = p.shape[0]
    st_ref[0] = p[length - 2:length, :]


def _conv_a(proj3, a_width, w_conv, hist4, layer, name):
    b, length, _ = proj3.shape
    tc = _pick(a_width, 512, V7X_LANES)
    na = a_width // tc
    has_hist = hist4 is not None
    in_specs = [pl.BlockSpec((1, length, tc), lambda bi, j: (bi, 0, j)),
                pl.BlockSpec((1, length, tc), lambda bi, j: (bi, 0, na + j)),
                pl.BlockSpec((1, length, tc), lambda bi, j: (bi, 0, 2 * na + j)),
                pl.BlockSpec((3, tc), lambda bi, j: (0, j))]
    args = [proj3, proj3, proj3, w_conv]
    if has_hist:
        in_specs.append(pl.BlockSpec((None, 1, 2, tc), lambda bi, j: (layer, bi, 0, j)))
        args.append(hist4)
    est = 2 * 3 * _nbytes((length, tc), f32) + 2 * _nbytes((length, tc), bf16) + 6 * _nbytes((length, tc), f32)
    return pl.pallas_call(
        functools.partial(_conva_kernel, has_hist=has_hist),
        grid=(b, na),
        in_specs=in_specs,
        out_specs=[pl.BlockSpec((1, length, tc), lambda bi, j: (bi, 0, j)),
                   pl.BlockSpec((1, 2, tc), lambda bi, j: (bi, 0, j))],
        out_shape=[jax.ShapeDtypeStruct((b, length, a_width), bf16),
                   jax.ShapeDtypeStruct((b, 2, a_width), f32)],
        compiler_params=_params(2, est),
        name=name,
    )(*args)


def _ffn_act_kernel(*refs, has_hist):
    if has_hist:
        ug_ref, uv_ref, wg_ref, wv_ref, bg_ref, bv_ref, hg_ref, hv_ref, act_ref, sg_ref, sv_ref = refs
        hg0, hg1 = hg_ref[0, 0:1, :], hg_ref[0, 1:2, :]
        hv0, hv1 = hv_ref[0, 0:1, :], hv_ref[0, 1:2, :]
    else:
        ug_ref, uv_ref, wg_ref, wv_ref, bg_ref, bv_ref, act_ref, sg_ref, sv_ref = refs
        hg0 = hg1 = hv0 = hv1 = jnp.zeros((1, ug_ref.shape[2]), f32)
    ug = ug_ref[0]
    uv = uv_ref[0]
    gate = _conv3(ug, hg0, hg1, wg_ref) + bg_ref[...]
    val = _conv3(uv, hv0, hv1, wv_ref) + bv_ref[...]
    act_ref[0] = (gate * jax.nn.sigmoid(gate) * val).astype(act_ref.dtype)
    length = ug.shape[0]
    sg_ref[0] = ug[length - 2:length, :]
    sv_ref[0] = uv[length - 2:length, :]


def _ffn_act(u3, d_ff, w_conv, b_conv, hist4, layer, name):
    b, length, _ = u3.shape
    tc = _pick(d_ff, 512, V7X_LANES)
    nf = d_ff // tc
    has_hist = hist4 is not None
    blk = lambda off: pl.BlockSpec((1, length, tc), lambda bi, j: (bi, 0, off + j))
    wblk = lambda off: pl.BlockSpec((3, tc), lambda bi, j: (0, off + j))
    bblk = lambda off: pl.BlockSpec((1, tc), lambda bi, j: (0, off + j))
    in_specs = [blk(0), blk(nf), wblk(0), wblk(nf), bblk(0), bblk(nf)]
    b2 = b_conv.reshape(1, 2 * d_ff)
    args = [u3, u3, w_conv, w_conv, b2, b2]
    if has_hist:
        hblk = lambda off: pl.BlockSpec((None, 1, 2, tc), lambda bi, j: (layer, bi, 0, off + j))
        in_specs += [hblk(0), hblk(nf)]
        args += [hist4, hist4]
    est = 2 * 2 * _nbytes((length, tc), f32) + 2 * _nbytes((length, tc), bf16) + 8 * _nbytes((length, tc), f32)
    act, sg, sv = pl.pallas_call(
        functools.partial(_ffn_act_kernel, has_hist=has_hist),
        grid=(b, nf),
        in_specs=in_specs,
        out_specs=[pl.BlockSpec((1, length, tc), lambda bi, j: (bi, 0, j)),
                   pl.BlockSpec((1, 2, tc), lambda bi, j: (bi, 0, j)),
                   pl.BlockSpec((1, 2, tc), lambda bi, j: (bi, 0, j))],
        out_shape=[jax.ShapeDtypeStruct((b, length, d_ff), bf16),
                   jax.ShapeDtypeStruct((b, 2, d_ff), f32),
                   jax.ShapeDtypeStruct((b, 2, d_ff), f32)],
        compiler_params=_params(2, est),
        name=name,
    )(*args)
    return act, jnp.concatenate([sg, sv], axis=-1)


def _ffn_up_act_kernel(x_ref, wg_ref, wv_ref, cg_ref, cv_ref, bg_ref, bv_ref, act_ref, sg_ref, sv_ref,
                       ug_ref, uv_ref, *, tiles_per_seq):
    i = pl.program_id(1)
    tm = x_ref.shape[0]
    pad = V7X_SUBLANES

    @pl.when(i % tiles_per_seq == 0)
    def _():
        ug_ref[0:pad, :] = jnp.zeros((pad, ug_ref.shape[1]), f32)
        uv_ref[0:pad, :] = jnp.zeros((pad, uv_ref.shape[1]), f32)

    x = x_ref[...]
    dims = (((1,), (0,)), ((), ()))
    ug_ref[pad:pad + tm, :] = lax.dot_general(x, wg_ref[...], dims, preferred_element_type=f32)
    uv_ref[pad:pad + tm, :] = lax.dot_general(x, wv_ref[...], dims, preferred_element_type=f32)

    def conv(u_ref, w_ref, b_ref):
        return (w_ref[0:1, :] * u_ref[pad - 2:pad - 2 + tm, :] + w_ref[1:2, :] * u_ref[pad - 1:pad - 1 + tm, :]
                + w_ref[2:3, :] * u_ref[pad:pad + tm, :] + b_ref[...])

    gate = conv(ug_ref, cg_ref, bg_ref)
    val = conv(uv_ref, cv_ref, bv_ref)
    act_ref[...] = (gate * jax.nn.sigmoid(gate) * val).astype(act_ref.dtype)
    tail_g = ug_ref[pad + tm - 2:pad + tm, :]
    tail_v = uv_ref[pad + tm - 2:pad + tm, :]
    ug_ref[pad - 2:pad, :] = tail_g
    uv_ref[pad - 2:pad, :] = tail_v
    sg_ref[0] = tail_g
    sv_ref[0] = tail_v


def _ffn_up_act(h, b, length, w_up, layer, d_ff, w_conv, b_conv, name):
    m, k = h.shape
    tm = _pick(length, 1024, 16)
    tc = _pick(d_ff, 256, V7X_LANES)
    nf = d_ff // tc
    tiles_per_seq = length // tm
    b2 = b_conv.reshape(1, 2 * d_ff)
    est = (2 * _nbytes((tm, k), h.dtype) + 2 * 2 * _nbytes((k, tc), f32) + 2 * _nbytes((k, tc), bf16)
           + 2 * _nbytes((tm, tc), bf16) + 12 * _nbytes((tm, tc), f32))
    act, sg, sv = pl.pallas_call(
        functools.partial(_ffn_up_act_kernel, tiles_per_seq=tiles_per_seq),
        grid=(nf, m // tm),
        in_specs=[pl.BlockSpec((tm, k), lambda j, i: (i, 0)),
                  pl.BlockSpec((None, k, tc), lambda j, i: (layer, 0, j)),
                  pl.BlockSpec((None, k, tc), lambda j, i: (layer, 0, nf + j)),
                  pl.BlockSpec((3, tc), lambda j, i: (0, j)),
                  pl.BlockSpec((3, tc), lambda j, i: (0, nf + j)),
                  pl.BlockSpec((1, tc), lambda j, i: (0, j)),
                  pl.BlockSpec((1, tc), lambda j, i: (0, nf + j))],
        out_specs=[pl.BlockSpec((tm, tc), lambda j, i: (i, j)),
                   pl.BlockSpec((1, 2, tc), lambda j, i: (i // tiles_per_seq, 0, j)),
                   pl.BlockSpec((1, 2, tc), lambda j, i: (i // tiles_per_seq, 0, j))],
        out_shape=[jax.ShapeDtypeStruct((m, d_ff), bf16),
                   jax.ShapeDtypeStruct((b, 2, d_ff), f32),
                   jax.ShapeDtypeStruct((b, 2, d_ff), f32)],
        scratch_shapes=[pltpu.VMEM((V7X_SUBLANES + tm, tc), f32),
                        pltpu.VMEM((V7X_SUBLANES + tm, tc), f32)],
        compiler_params=_params(2, est),
        name=name,
    )(h, w_up, w_up, w_conv, w_conv, b2, b2)
    return act, jnp.concatenate([sg, sv], axis=-1)


def _gla_kernel(*refs, chunk, n_chunks, scale, has_s0):
    if has_s0:
        q_ref, k_ref, v_ref, r_ref, gl_ref, wg_ref, bg_ref, gn_ref, s0_ref, y_ref, s_ref, st_ref = refs
    else:
        q_ref, k_ref, v_ref, r_ref, gl_ref, wg_ref, bg_ref, gn_ref, y_ref, s_ref, st_ref = refs
    t = pl.program_id(2)

    @pl.when(t == 0)
    def _():
        if has_s0:
            st_ref[...] = s0_ref[0, 0].T
        else:
            st_ref[...] = jnp.zeros(st_ref.shape, f32)

    ri = lax.broadcasted_iota(jnp.int32, (chunk, chunk), 0)
    ci = lax.broadcasted_iota(jnp.int32, (chunk, chunk), 1)
    causal = ri >= ci
    tri = causal.astype(f32)
    nt = (((1,), (1,)), ((), ()))
    tn = (((0,), (0,)), ((), ()))

    def body(c, carry):
        sl = pl.ds(pl.multiple_of(c * chunk, chunk), chunk)
        q = q_ref[0, sl, :] * scale
        k = k_ref[0, sl, :]
        v = v_ref[0, sl, :]
        pre = jnp.dot(gl_ref[0, sl, :], wg_ref[...], preferred_element_type=f32) + bg_ref[...]
        log_a = _log_sigmoid(pre) / GLA_TAU
        bcum = jnp.dot(tri, log_a, precision=HIGHEST, preferred_element_type=f32)
        b_last = bcum[chunk - 1:chunk, :]
        q_dec = q * jnp.exp(bcum)
        k_inv = k * jnp.exp(-bcum)
        k_tail = k * jnp.exp(b_last - bcum)
        att = lax.dot_general(q_dec, k_inv, nt, preferred_element_type=f32)
        att = jnp.where(causal, att, 0.0)
        st = st_ref[...]
        o = (lax.dot_general(q_dec, st, nt, preferred_element_type=f32)
             + jnp.dot(att, v, preferred_element_type=f32))
        st_ref[...] = jnp.exp(b_last) * st + lax.dot_general(v, k_tail, tn, preferred_element_type=f32)
        on = o * lax.rsqrt(jnp.mean(o * o, axis=-1, keepdims=True) + EPS) * gn_ref[...]
        r = r_ref[0, sl, :]
        y_ref[0, sl, :] = (on * (r * jax.nn.sigmoid(r))).astype(y_ref.dtype)
        return carry

    lax.fori_loop(0, n_chunks, body, 0)

    @pl.when(t == pl.num_programs(2) - 1)
    def _():
        s_ref[0, 0] = st_ref[...].T


def _gla(proj3, g_low3, offs, dims, w_gate_up, b_gate, g_norm, s0_5, layer, name):
    b, length, _ = proj3.shape
    heads, dk, dv = dims
    q_off, k_off, v_off, r_off = offs
    rank = g_low3.shape[-1]
    chunk = _gcd(length, GLA_CHUNK)
    tl = _pick(length, 512, chunk)
    n_chunks = tl // chunk
    assert q_off % dk == 0 and k_off % dk == 0 and v_off % dv == 0 and r_off % dv == 0
    qb, kb, vb, rb = q_off // dk, k_off // dk, v_off // dv, r_off // dv
    has_s0 = s0_5 is not None
    in_specs = [pl.BlockSpec((1, tl, dk), lambda bi, h, t: (bi, t, qb + h)),
                pl.BlockSpec((1, tl, dk), lambda bi, h, t: (bi, t, kb + h)),
                pl.BlockSpec((1, tl, dv), lambda bi, h, t: (bi, t, vb + h)),
                pl.BlockSpec((1, tl, dv), lambda bi, h, t: (bi, t, rb + h)),
                pl.BlockSpec((1, tl, rank), lambda bi, h, t: (bi, t, 0)),
                pl.BlockSpec((rank, dk), lambda bi, h, t: (0, h)),
                pl.BlockSpec((1, dk), lambda bi, h, t: (0, h)),
                pl.BlockSpec((1, dv), lambda bi, h, t: (0, h))]
    args = [proj3, proj3, proj3, proj3, g_low3, w_gate_up, b_gate.reshape(1, -1), g_norm.reshape(1, -1)]
    if has_s0:
        in_specs.append(pl.BlockSpec((None, 1, 1, dk, dv), lambda bi, h, t: (layer, bi, h, 0, 0)))
        args.append(s0_5)
    est = (2 * (2 * _nbytes((tl, dk), f32) + 2 * _nbytes((tl, dv), f32) + _nbytes((tl, V7X_LANES), f32))
           + 2 * _nbytes((tl, dv), bf16) + 5 * _nbytes((dk, dv), f32) + 16 * _nbytes((chunk, dv), f32))
    return pl.pallas_call(
        functools.partial(_gla_kernel, chunk=chunk, n_chunks=n_chunks, scale=dk ** -0.5, has_s0=has_s0),
        grid=(b, heads, length // tl),
        in_specs=in_specs,
        out_specs=[pl.BlockSpec((1, tl, dv), lambda bi, h, t: (bi, t, h)),
                   pl.BlockSpec((1, 1, dk, dv), lambda bi, h, t: (bi, h, 0, 0))],
        out_shape=[jax.ShapeDtypeStruct((b, length, heads * dv), bf16),
                   jax.ShapeDtypeStruct((b, heads, dk, dv), f32)],
        scratch_shapes=[pltpu.VMEM((dv, dk), f32)],
        compiler_params=_params(3, est),
        name=name,
    )(*args)


def _gcd(a, b):
    while b:
        a, b = b, a % b
    return a


def _logf_kernel(fl_ref, bf_ref, lf_ref, c_ref):
    lf = _log_sigmoid(fl_ref[0] + bf_ref[...])
    lf_ref[0] = lf
    c = lf
    row = lax.broadcasted_iota(jnp.int32, c.shape, 0)
    s = 1
    while s < c.shape[0]:
        c = c + jnp.where(row >= s, pltpu.roll(c, s, 0), 0.0)
        s *= 2
    c_ref[0] = c


def _logf_cumsum(fl3, b_f, name):
    b, length, h = fl3.shape
    est = 16 * _nbytes((length, V7X_LANES), f32)
    return pl.pallas_call(
        _logf_kernel,
        grid=(b,),
        in_specs=[pl.BlockSpec((1, length, h), lambda bi: (bi, 0, 0)),
                  pl.BlockSpec((1, h), lambda bi: (0, 0))],
        out_specs=[pl.BlockSpec((1, length, h), lambda bi: (bi, 0, 0)),
                   pl.BlockSpec((1, length, h), lambda bi: (bi, 0, 0))],
        out_shape=[jax.ShapeDtypeStruct((b, length, h), f32),
                   jax.ShapeDtypeStruct((b, length, h), f32)],
        compiler_params=_params(1, est),
        name=name,
    )(fl3, b_f.reshape(1, h))


def _flash_kernel(q_ref, k_ref, v_ref, c_ref, o_ref, crow_ref, *, tq, n_heads, scale):
    h = pl.program_id(1)
    qi = pl.program_id(2)
    nt = (((1,), (1,)), ((), ()))
    dh = q_ref.shape[2]

    sel_col = (lax.broadcasted_iota(jnp.int32, (n_heads, V7X_LANES), 0) == h).astype(f32)
    sel_row = (lax.broadcasted_iota(jnp.int32, (V7X_SUBLANES, n_heads), 1) == h).astype(f32)
    q0 = pl.multiple_of(qi * tq, tq)
    c_q = jnp.dot(c_ref[0, pl.ds(q0, tq), :], sel_col, precision=HIGHEST, preferred_element_type=f32)
    c_q = jnp.tile(c_q, (1, tq // V7X_LANES))

    @pl.when(qi == 0)
    def _():
        crow_ref[...] = lax.dot_general(sel_row, c_ref[0], nt, precision=HIGHEST, preferred_element_type=f32)

    q = q_ref[0] * scale

    def step(j, carry, masked):
        m, l, acc = carry
        k0 = pl.multiple_of(j * tq, tq)
        kb = k_ref[0, pl.ds(k0, tq), :]
        vb = v_ref[0, pl.ds(k0, tq), :]
        s = lax.dot_general(q, kb, nt, preferred_element_type=f32)
        s = s + (c_q - crow_ref[0:1, pl.ds(k0, tq)])
        if masked:
            ri = lax.broadcasted_iota(jnp.int32, (tq, tq), 0)
            ci = lax.broadcasted_iota(jnp.int32, (tq, tq), 1)
            s = jnp.where(ci <= ri, s, -jnp.inf)
        m_new = jnp.maximum(m, jnp.max(s, axis=-1, keepdims=True))
        p = jnp.exp(s - m_new)
        alpha = jnp.exp(m - m_new)
        l = alpha * l + jnp.sum(p, axis=-1, keepdims=True)
        acc = alpha * acc + jnp.dot(p, vb, preferred_element_type=f32)
        return m_new, l, acc

    init = (jnp.full((tq, 1), -jnp.inf, f32), jnp.zeros((tq, 1), f32), jnp.zeros((tq, dh), f32))
    carry = lax.fori_loop(0, qi, functools.partial(step, masked=False), init)
    _, l, acc = step(qi, carry, True)
    o_ref[0] = (acc / l).astype(o_ref.dtype)


def _flash(q3, k3, v3, c3, n_heads, name):
    b, length, d = q3.shape
    dh = d // n_heads
    tq = _pick(length, 512, V7X_LANES)
    est = (2 * _nbytes((tq, dh), f32) + 4 * _nbytes((length, dh), f32) + 2 * _nbytes((length, V7X_LANES), f32)
           + 2 * _nbytes((tq, dh), bf16) + _nbytes((V7X_SUBLANES, length), f32) + 8 * _nbytes((tq, tq), f32))
    return pl.pallas_call(
        functools.partial(_flash_kernel, tq=tq, n_heads=n_heads, scale=dh ** -0.5),
        grid=(b, n_heads, length // tq),
        in_specs=[pl.BlockSpec((1, tq, dh), lambda bi, h, qi: (bi, qi, h)),
                  pl.BlockSpec((1, length, dh), lambda bi, h, qi: (bi, 0, h)),
                  pl.BlockSpec((1, length, dh), lambda bi, h, qi: (bi, 0, h)),
                  pl.BlockSpec((1, length, n_heads), lambda bi, h, qi: (bi, 0, 0))],
        out_specs=pl.BlockSpec((1, tq, dh), lambda bi, h, qi: (bi, qi, h)),
        out_shape=jax.ShapeDtypeStruct((b, length, d), bf16),
        scratch_shapes=[pltpu.VMEM((V7X_SUBLANES, length), f32)],
        compiler_params=_params(3, est),
        name=name,
    )(q3, k3, v3, c3)


def _decode_logits_kernel(pt_ref, q_ref, kn_ref, lfs_ref, kp_ref, fp_ref, sp_ref, sn_ref,
                          qbd_ref, cnew_ref, carry_ref, *, n_heads, scale):
    del pt_ref
    p = pl.program_id(1)
    lq = q_ref.shape[1]
    d = q_ref.shape[2]
    dh = d // n_heads
    nq = n_heads * lq
    page = kp_ref.shape[0]
    expand = ((lax.broadcasted_iota(jnp.int32, (n_heads, nq), 1) // lq)
              == lax.broadcasted_iota(jnp.int32, (n_heads, nq), 0)).astype(f32)

    @pl.when(p == 0)
    def _():
        qt = jnp.tile(q_ref[0] * scale, (n_heads, 1))
        rh = lax.broadcasted_iota(jnp.int32, (nq, d), 0) // lq
        ch = lax.broadcasted_iota(jnp.int32, (nq, d), 1) // dh
        qbd = jnp.where(rh == ch, qt, 0.0).T
        qbd_ref[...] = qbd.astype(qbd_ref.dtype)
        x = jnp.dot(lfs_ref[0], expand, precision=HIGHEST, preferred_element_type=f32)
        tk = lax.broadcasted_iota(jnp.int32, (lq, nq), 0)
        tq = lax.broadcasted_iota(jnp.int32, (lq, nq), 1) % lq
        c_query = jnp.sum(jnp.where(tk <= tq, x, 0.0), axis=0, keepdims=True)
        tri = (lax.broadcasted_iota(jnp.int32, (lq, lq), 0)
               >= lax.broadcasted_iota(jnp.int32, (lq, lq), 1)).astype(f32)
        c_key = jnp.dot(tri, x, precision=HIGHEST, preferred_element_type=f32)
        cnew_ref[...] = c_query
        carry_ref[...] = jnp.zeros(carry_ref.shape, f32)
        s_new = jnp.dot(kn_ref[0].astype(qbd_ref.dtype), qbd_ref[...], preferred_element_type=f32)
        s_new = s_new + (c_query - c_key)
        sn_ref[0] = jnp.where(tk <= tq, s_new, -jnp.inf)

    fpage = fp_ref[...]
    upper = (lax.broadcasted_iota(jnp.int32, (page, page), 1)
             > lax.broadcasted_iota(jnp.int32, (page, page), 0)).astype(f32)
    suffix = jnp.dot(upper, fpage, precision=HIGHEST, preferred_element_type=f32) + carry_ref[...]
    carry_ref[...] = carry_ref[...] + jnp.sum(fpage, axis=0, keepdims=True)
    bias = jnp.dot(suffix, expand, precision=HIGHEST, preferred_element_type=f32) + cnew_ref[...]
    s = jnp.dot(kp_ref[...].astype(qbd_ref.dtype), qbd_ref[...], preferred_element_type=f32)
    sp_ref[0] = s + bias


def _decode_logits(q3, kn3, lfs3, cache_k4, cache_f4, page_table, layer, n_heads, name):
    b, lq, d = q3.shape
    n_pages = page_table.shape[1]
    page = cache_k4.shape[2]
    nq = n_heads * lq
    dh = d // n_heads
    last = n_pages - 1
    grid_spec = pltpu.PrefetchScalarGridSpec(
        num_scalar_prefetch=1,
        grid=(b, n_pages),
        in_specs=[pl.BlockSpec((1, lq, d), lambda bi, p, pt: (bi, 0, 0)),
                  pl.BlockSpec((1, lq, d), lambda bi, p, pt: (bi, 0, 0)),
                  pl.BlockSpec((1, lq, n_heads), lambda bi, p, pt: (bi, 0, 0)),
                  pl.BlockSpec((None, None, page, d), lambda bi, p, pt: (layer, pt[bi, last - p], 0, 0)),
                  pl.BlockSpec((None, None, page, n_heads), lambda bi, p, pt: (layer, pt[bi, last - p], 0, 0))],
        out_specs=[pl.BlockSpec((1, page, nq), lambda bi, p, pt: (bi, last - p, 0)),
                   pl.BlockSpec((1, lq, nq), lambda bi, p, pt: (bi, 0, 0))],
        scratch_shapes=[pltpu.VMEM((d, nq), bf16),
                        pltpu.VMEM((1, nq), f32),
                        pltpu.VMEM((1, n_heads), f32)],
    )
    est = (2 * _nbytes((page, d), f32) + _nbytes((page, d), bf16) + 3 * _nbytes((d, nq), f32)
           + 8 * _nbytes((page, nq), f32) + 6 * _nbytes((lq, d), f32))
    return pl.pallas_call(
        functools.partial(_decode_logits_kernel, n_heads=n_heads, scale=dh ** -0.5),
        grid_spec=grid_spec,
        out_shape=[jax.ShapeDtypeStruct((b, n_pages * page, nq), f32),
                   jax.ShapeDtypeStruct((b, lq, nq), f32)],
        compiler_params=_params(2, est),
        name=name,
    )(page_table, q3, kn3, lfs3, cache_k4, cache_f4)


def _decode_pv_kernel(pt_ref, sp_ref, sn_ref, vn_ref, vp_ref, o_ref, prob_ref, acc_ref, *, n_heads):
    del pt_ref
    p = pl.program_id(1)
    lq = vn_ref.shape[1]
    d = vn_ref.shape[2]
    dh = d // n_heads
    page = vp_ref.shape[0]
    tn = (((0,), (0,)), ((), ()))

    @pl.when(p == 0)
    def _():
        sp = sp_ref[0]
        sn = sn_ref[0]
        m = jnp.maximum(jnp.max(sp, axis=0, keepdims=True), jnp.max(sn, axis=0, keepdims=True))
        ep = jnp.exp(sp - m)
        en = jnp.exp(sn - m)
        inv = 1.0 / (jnp.sum(ep, axis=0, keepdims=True) + jnp.sum(en, axis=0, keepdims=True))
        prob_ref[...] = ep * inv
        acc_ref[...] = lax.dot_general(en * inv, vn_ref[0], tn, preferred_element_type=f32)

    rows = pl.ds(pl.multiple_of(p * page, page), page)
    acc_ref[...] += lax.dot_general(prob_ref[rows, :], vp_ref[...], tn, preferred_element_type=f32)

    @pl.when(p == pl.num_programs(1) - 1)
    def _():
        parts = [acc_ref[h * lq:(h + 1) * lq, h * dh:(h + 1) * dh] for h in range(n_heads)]
        o_ref[0] = jnp.concatenate(parts, axis=1).astype(o_ref.dtype)


def _decode_pv(sp, sn, vn3, cache_v4, page_table, layer, n_heads, name):
    b, lq, d = vn3.shape
    n_pages = page_table.shape[1]
    page = cache_v4.shape[2]
    past = n_pages * page
    nq = n_heads * lq
    grid_spec = pltpu.PrefetchScalarGridSpec(
        num_scalar_prefetch=1,
        grid=(b, n_pages),
        in_specs=[pl.BlockSpec((1, past, nq), lambda bi, p, pt: (bi, 0, 0)),
                  pl.BlockSpec((1, lq, nq), lambda bi, p, pt: (bi, 0, 0)),
                  pl.BlockSpec((1, lq, d), lambda bi, p, pt: (bi, 0, 0)),
                  pl.BlockSpec((None, None, page, d), lambda bi, p, pt: (layer, pt[bi, p], 0, 0))],
        out_specs=pl.BlockSpec((1, lq, d), lambda bi, p, pt: (bi, 0, 0)),
        scratch_shapes=[pltpu.VMEM((past, nq), f32),
                        pltpu.VMEM((nq, d), f32)],
    )
    est = (2 * _nbytes((past, nq), f32) + 3 * _nbytes((past, nq), f32) + 2 * _nbytes((page, d), f32)
           + _nbytes((page, d), bf16) + 3 * _nbytes((nq, d), f32))
    return pl.pallas_call(
        functools.partial(_decode_pv_kernel, n_heads=n_heads),
        grid_spec=grid_spec,
        out_shape=jax.ShapeDtypeStruct((b, lq, d), bf16),
        compiler_params=_params(2, est),
        name=name,
    )(page_table, sp, sn, vn3, cache_v4)


def _decode_kernel(pt_ref, q_ref, kn_ref, vn_ref, lfs_ref, kp_ref, vp_ref, fp_ref, o_ref,
                   qs_ref, m_ref, l_ref, acc_ref, carry_ref, *, scale):
    del pt_ref
    p = pl.program_id(1)
    lq, n_heads, dh = q_ref.shape[1], q_ref.shape[2], q_ref.shape[3]
    gh = V7X_SUBLANES
    n_groups = n_heads // gh
    nq = lq * gh
    nt = (((1,), (1,)), ((), ()))
    tn = (((0,), (0,)), ((), ()))
    s_i = lax.broadcasted_iota(jnp.int32, (gh, V7X_LANES), 0)
    c_i = lax.broadcasted_iota(jnp.int32, (gh, V7X_LANES), 1)
    diag = ((c_i % gh) == s_i) & (c_i < nq)

    def attend(k3, v3, f2, causal):
        nk = f2.shape[0]
        upper = (lax.broadcasted_iota(jnp.int32, (nk, nk), 1)
                 > lax.broadcasted_iota(jnp.int32, (nk, nk), 0)).astype(f32)
        beta = jnp.dot(upper, f2, precision=HIGHEST, preferred_element_type=f32) + carry_ref[...]
        carry_ref[...] = carry_ref[...] + jnp.sum(f2, axis=0, keepdims=True)
        keep = diag[None]
        if causal:
            key_i = lax.broadcasted_iota(jnp.int32, (nk, gh, V7X_LANES), 0)
            t_i = lax.broadcasted_iota(jnp.int32, (nk, gh, V7X_LANES), 2) // gh
            keep = keep & (key_i <= t_i)
        for a in range(n_groups):
            hs = slice(a * gh, (a + 1) * gh)
            ka = k3[:, hs, :].reshape(nk * gh, dh)
            va = v3[:, hs, :].reshape(nk * gh, dh)
            g = lax.dot_general(ka, qs_ref[a], nt, preferred_element_type=f32)
            pick = ((lax.broadcasted_iota(jnp.int32, (n_heads, V7X_LANES), 0)
                     == a * gh + lax.broadcasted_iota(jnp.int32, (n_heads, V7X_LANES), 1) % gh)
                    & (lax.broadcasted_iota(jnp.int32, (n_heads, V7X_LANES), 1) < nq)).astype(f32)
            xa = jnp.dot(beta, pick, precision=HIGHEST, preferred_element_type=f32)
            g3 = g.reshape(nk, gh, V7X_LANES) + xa[:, None, :]
            g3 = jnp.where(keep, g3, -jnp.inf)
            m_old = m_ref[a]
            m_new = jnp.maximum(m_old, jnp.max(g3, axis=(0, 1)).reshape(1, V7X_LANES))
            m_safe = jnp.where(m_new == -jnp.inf, 0.0, m_new)
            alpha = jnp.exp(m_old - m_safe)
            p3 = jnp.exp(g3 - m_safe[None])
            l_ref[a] = alpha * l_ref[a] + jnp.sum(p3, axis=(0, 1)).reshape(1, V7X_LANES)
            acc_ref[a] = alpha * acc_ref[a] + lax.dot_general(
                va, p3.reshape(nk * gh, V7X_LANES), tn, preferred_element_type=f32)
            m_ref[a] = m_new

    @pl.when(p == 0)
    def _():
        for a in range(n_groups):
            qa = q_ref[0, :, a * gh:(a + 1) * gh, :].reshape(nq, dh) * scale
            qs_ref[a] = jnp.concatenate([qa, jnp.zeros((V7X_LANES - nq, dh), f32)], axis=0)
        m_ref[...] = jnp.full(m_ref.shape, -jnp.inf, f32)
        l_ref[...] = jnp.zeros(l_ref.shape, f32)
        acc_ref[...] = jnp.zeros(acc_ref.shape, f32)
        carry_ref[...] = jnp.zeros(carry_ref.shape, f32)
        attend(kn_ref[0], vn_ref[0], lfs_ref[0], True)

    @pl.when(p > 0)
    def _():
        attend(kp_ref, vp_ref, fp_ref[...], False)

    @pl.when(p == pl.num_programs(1) - 1)
    def _():
        for a in range(n_groups):
            out_t = acc_ref[a] / l_ref[a]
            out = out_t.T
            for t in range(lq):
                o_ref[0, t, a * gh:(a + 1) * gh, :] = out[t * gh:(t + 1) * gh, :]


def _decode_attention(q4, kn4, vn4, lfs3, cache_k, cache_v, cache_f, page_table, layer, name):
    b, lq, n_heads, dh = q4.shape
    n_pages = page_table.shape[1]
    page = cache_k.shape[2]
    assert n_heads % V7X_SUBLANES == 0 and lq * V7X_SUBLANES <= V7X_LANES and dh == V7X_LANES
    n_groups = n_heads // V7X_SUBLANES
    last = n_pages - 1
    page_of = lambda bi, p, pt: pt[bi, last - jnp.maximum(p - 1, 0)]
    new_blk = lambda bi, p, pt: (bi, 0, 0, 0)
    grid_spec = pltpu.PrefetchScalarGridSpec(
        num_scalar_prefetch=1,
        grid=(b, n_pages + 1),
        in_specs=[pl.BlockSpec((1, lq, n_heads, dh), new_blk),
                  pl.BlockSpec((1, lq, n_heads, dh), new_blk),
                  pl.BlockSpec((1, lq, n_heads, dh), new_blk),
                  pl.BlockSpec((1, lq, n_heads), lambda bi, p, pt: (bi, 0, 0)),
                  pl.BlockSpec((None, None, page, n_heads, dh),
                               lambda bi, p, pt: (layer, page_of(bi, p, pt), 0, 0, 0)),
                  pl.BlockSpec((None, None, page, n_heads, dh),
                               lambda bi, p, pt: (layer, page_of(bi, p, pt), 0, 0, 0)),
                  pl.BlockSpec((None, None, page, n_heads),
                               lambda bi, p, pt: (layer, page_of(bi, p, pt), 0, 0))],
        out_specs=pl.BlockSpec((1, lq, n_heads, dh), new_blk),
        scratch_shapes=[pltpu.VMEM((n_groups, V7X_LANES, dh), f32),
                        pltpu.VMEM((n_groups, 1, V7X_LANES), f32),
                        pltpu.VMEM((n_groups, 1, V7X_LANES), f32),
                        pltpu.VMEM((n_groups, dh, V7X_LANES), f32),
                        pltpu.VMEM((1, n_heads), f32)],
    )
    est = (4 * _nbytes((page, n_heads, dh), f32) + 8 * _nbytes((lq, n_heads, dh), f32)
           + 8 * _nbytes((page * V7X_SUBLANES, V7X_LANES), f32) + 4 * _nbytes((n_groups, V7X_LANES, dh), f32))
    return pl.pallas_call(
        functools.partial(_decode_kernel, scale=dh ** -0.5),
        grid_spec=grid_spec,
        out_shape=jax.ShapeDtypeStruct((b, lq, n_heads, dh), f32),
        compiler_params=_params(2, est),
        name=name,
    )(page_table, q4, kn4, vn4, lfs3, cache_k, cache_v, cache_f)


def _even_mixer(h, b, length, e, tag, heads, w_in_even, w_conv_a, w_gate_up, b_gate, g_gla_norm, w_out_even,
                state_conv_a, state_gla):
    a_width = w_conv_a.shape[-1]
    kdim = w_gate_up.shape[-1]
    rank = w_gate_up.shape[1]
    vdim = g_gla_norm.shape[-1]
    dk, dv = kdim // heads, vdim // heads
    wide = 3 * a_width + 2 * kdim + 2 * vdim
    proj = _mm(h, w_in_even, e, 0, wide, tm_target=1024, tn_target=512, name=f"in_even_{tag}")
    g_low = _mm_tail(h, w_in_even[e, :, wide:wide + rank], name=f"in_even_gate_{tag}")
    proj3 = proj.reshape(b, length, wide)
    y_a, conv_state = _conv_a(proj3, a_width, w_conv_a[e], state_conv_a, e, name=f"conv_a_{tag}")
    q_off = 3 * a_width
    offs = (q_off, q_off + kdim, q_off + 2 * kdim, q_off + 2 * kdim + vdim)
    y_b, s_new = _gla(proj3, g_low.reshape(b, length, rank), offs, (heads, dk, dv),
                      w_gate_up[e], b_gate[e], g_gla_norm[e], state_gla, e, name=f"gla_{tag}")
    y = jnp.concatenate([y_a, y_b], axis=-1).reshape(b * length, a_width + vdim)
    mix = _mm(y, w_out_even, e, 0, w_out_even.shape[-1], tm_target=1024, tn_target=512,
              name=f"out_even_{tag}")
    return mix, conv_state, s_new


def _fox_qkv(h, b, length, o, tag, w_in_odd, b_forget):
    d = h.shape[1]
    n_heads = b_forget.shape[-1]
    q = _mm(h, w_in_odd, o, 0, d, tm_target=1024, tn_target=512, name=f"fox_q_{tag}")
    k = _mm(h, w_in_odd, o, d, d, tm_target=1024, tn_target=512, name=f"fox_k_{tag}")
    v = _mm(h, w_in_odd, o, 2 * d, d, tm_target=1024, tn_target=512, name=f"fox_v_{tag}")
    fl = _mm_tail(h, w_in_odd[o, :, 3 * d:3 * d + n_heads], name=f"fox_f_{tag}")
    logf, csum = _logf_cumsum(fl.reshape(b, length, n_heads), b_forget[o], name=f"fox_logf_{tag}")
    return (q.reshape(b, length, d), k.reshape(b, length, d), v.reshape(b, length, d), logf, csum)


def _ffn(h, b, length, l, tag, w_up, w_ffn_conv, b_ffn_conv, w_down, state_ffn_conv):
    d_ff = w_down.shape[1]
    if state_ffn_conv is None:
        act, hist = _ffn_up_act(h, b, length, w_up, l, d_ff, w_ffn_conv[l], b_ffn_conv[l],
                                name=f"ffn_up_{tag}")
    else:
        u = _mm(h, w_up, l, 0, 2 * d_ff, tm_target=1024, tn_target=512, name=f"ffn_up_{tag}")
        act, hist = _ffn_act(u.reshape(b, length, 2 * d_ff), d_ff, w_ffn_conv[l], b_ffn_conv[l],
                             state_ffn_conv, l, name=f"ffn_act_{tag}")
        act = act.reshape(b * length, d_ff)
    down = _mm(act, w_down, l, 0, w_down.shape[-1],
               tm_target=256, tn_target=512, w_single_buffer=True, name=f"ffn_down_{tag}")
    return down, hist


def kernel(x_prompt, x_sample, state_conv_a, state_gla, cache_k, cache_v, cache_logf, state_ffn_conv, page_table, g_mix_pre, g_mix_post, g_ffn_pre, g_ffn_post, w_in_even, w_conv_a, w_gate_up, b_gate, g_gla_norm, w_out_even, w_in_odd, b_forget, w_out_odd, w_up, w_ffn_conv, b_ffn_conv, w_down):
    bp, lp, d = x_prompt.shape
    bs, ls, _ = x_sample.shape
    depth = g_mix_pre.shape[0]
    n_heads = cache_k.shape[3]
    gla_heads = state_gla.shape[2]

    xp = x_prompt.reshape(bp * lp, d)
    xs = x_sample.reshape(bs * ls, d)
    ca_p, ca_s, sg_p, sg_s = [], [], [], []
    kp_l, vp_l, lp_l, ks_l, vs_l, ls_l = [], [], [], [], [], []
    fc_p, fc_s = [], []
    even_w = (w_in_even, w_conv_a, w_gate_up, b_gate, g_gla_norm, w_out_even)
    ffn_w = (w_up, w_ffn_conv, b_ffn_conv, w_down)
    hp = _rms_cast(xp, g_mix_pre[0], name="mix_pre_p0")
    hs = _rms_cast(xs, g_mix_pre[0], name="mix_pre_s0")
    for l in range(depth):
        if l % 2 == 0:
            e = l // 2
            mp, c_p, s_p = _even_mixer(hp, bp, lp, e, f"p{l}", gla_heads, *even_w, None, None)
            ms, c_s, s_s = _even_mixer(hs, bs, ls, e, f"s{l}", gla_heads, *even_w, state_conv_a, state_gla)
            ca_p.append(c_p)
            ca_s.append(c_s)
            sg_p.append(s_p)
            sg_s.append(s_s)
        else:
            o = l // 2
            qp, kp, vp, lfp, cp = _fox_qkv(hp, bp, lp, o, f"p{l}", w_in_odd, b_forget)
            ap = _flash(qp, kp, vp, cp, n_heads, name=f"fox_attn_p{l}")
            mp = _mm(ap.reshape(bp * lp, d), w_out_odd, o, 0, d, tm_target=1024, tn_target=512,
                     name=f"fox_out_p{l}")
            qs, kn, vn, lfs, _ = _fox_qkv(hs, bs, ls, o, f"s{l}", w_in_odd, b_forget)
            head_shape = (bs, ls, n_heads, d // n_heads)
            a_s = _decode_attention(qs.reshape(head_shape), kn.reshape(head_shape), vn.reshape(head_shape),
                                    lfs, cache_k, cache_v, cache_logf, page_table, o, name=f"fox_attn_s{l}")
            ms = _mm(a_s.reshape(bs * ls, d).astype(bf16), w_out_odd, o, 0, d, tm_target=1024, tn_target=512,
                     name=f"fox_out_s{l}")
            kp_l.append(kp.reshape(bp, lp, n_heads, d // n_heads))
            vp_l.append(vp.reshape(bp, lp, n_heads, d // n_heads))
            lp_l.append(lfp)
            ks_l.append(kn.reshape(bs, ls, n_heads, d // n_heads))
            vs_l.append(vn.reshape(bs, ls, n_heads, d // n_heads))
            ls_l.append(lfs)
        xp, hp = _post_pre(mp, g_mix_post[l], xp, g_ffn_pre[l], name=f"mix_post_p{l}")
        xs, hs = _post_pre(ms, g_mix_post[l], xs, g_ffn_pre[l], name=f"mix_post_s{l}")
        fp, h_p = _ffn(hp, bp, lp, l, f"p{l}", *ffn_w, None)
        fs, h_s = _ffn(hs, bs, ls, l, f"s{l}", *ffn_w, state_ffn_conv)
        fc_p.append(h_p)
        fc_s.append(h_s)
        if l + 1 < depth:
            xp, hp = _post_pre(fp, g_ffn_post[l], xp, g_mix_pre[l + 1], name=f"ffn_post_p{l}")
            xs, hs = _post_pre(fs, g_ffn_post[l], xs, g_mix_pre[l + 1], name=f"ffn_post_s{l}")
        else:
            xp = _post_residual(fp, g_ffn_post[l], xp, name=f"ffn_post_p{l}")
            xs = _post_residual(fs, g_ffn_post[l], xs, name=f"ffn_post_s{l}")

    return (xp.reshape(bp, lp, d), xs.reshape(bs, ls, d),
            jnp.stack(ca_p), jnp.stack(ca_s), jnp.stack(sg_p), jnp.stack(sg_s),
            jnp.stack(kp_l), jnp.stack(vp_l), jnp.stack(lp_l),
            jnp.stack(ks_l), jnp.stack(vs_l), jnp.stack(ls_l),
            jnp.stack(fc_p), jnp.stack(fc_s))
```

```python
import functools

import jax
import jax.numpy as jnp
from jax import lax
from jax.experimental import pallas as pl
from jax.experimental.pallas import tpu as pltpu

f32 = jnp.float32
bf16 = jnp.bfloat16

EPS = 1e-6
GLA_TAU = 16.0
GLA_CHUNK = 64
HIGHEST = lax.Precision.HIGHEST

V7X_LANES = 128
V7X_SUBLANES = 8
V7X_VMEM_BYTES = 64 * 1024 * 1024
VMEM_CAP = V7X_VMEM_BYTES - 6 * 1024 * 1024


def _pick(n, target, align):
    t = (min(target, n) // align) * align
    while t >= align:
        if n % t == 0:
            return t
        t -= align
    return n


def _params(n_axes, vmem_estimate):
    limit = min(VMEM_CAP, max(32 * 1024 * 1024, int(vmem_estimate * 1.25)))
    return pltpu.CompilerParams(dimension_semantics=("arbitrary",) * n_axes,
                                vmem_limit_bytes=limit)


def _nbytes(shape, dtype):
    n = 1
    for s in shape:
        n *= s
    return n * jnp.dtype(dtype).itemsize


def _log_sigmoid(x):
    return jnp.minimum(x, 0.0) - jnp.log1p(jnp.exp(-jnp.abs(x)))


def _rms_kernel(x_ref, g_ref, o_ref):
    x = x_ref[...]
    y = x * lax.rsqrt(jnp.mean(x * x, axis=-1, keepdims=True) + EPS)
    o_ref[...] = (y * g_ref[...]).astype(o_ref.dtype)


def _rms_cast(x2, g, name):
    m, d = x2.shape
    tm = _pick(m, 256, 16)
    est = 2 * (_nbytes((tm, d), f32) + _nbytes((tm, d), bf16)) + 4 * _nbytes((tm, d), f32)
    return pl.pallas_call(
        _rms_kernel,
        grid=(m // tm,),
        in_specs=[pl.BlockSpec((tm, d), lambda i: (i, 0)),
                  pl.BlockSpec((1, d), lambda i: (0, 0))],
        out_specs=pl.BlockSpec((tm, d), lambda i: (i, 0)),
        out_shape=jax.ShapeDtypeStruct((m, d), bf16),
        compiler_params=_params(1, est),
        name=name,
    )(x2, g.reshape(1, d))


def _post_kernel(m_ref, g_ref, x_ref, o_ref):
    m = m_ref[...]
    y = m * lax.rsqrt(jnp.mean(m * m, axis=-1, keepdims=True) + EPS)
    o_ref[...] = x_ref[...] + y * g_ref[...]


def _post_residual(m2, g, x2, name):
    m, d = x2.shape
    tm = _pick(m, 256, 8)
    est = 2 * 3 * _nbytes((tm, d), f32) + 4 * _nbytes((tm, d), f32)
    return pl.pallas_call(
        _post_kernel,
        grid=(m // tm,),
        in_specs=[pl.BlockSpec((tm, d), lambda i: (i, 0)),
                  pl.BlockSpec((1, d), lambda i: (0, 0)),
                  pl.BlockSpec((tm, d), lambda i: (i, 0))],
        out_specs=pl.BlockSpec((tm, d), lambda i: (i, 0)),
        out_shape=jax.ShapeDtypeStruct((m, d), f32),
        compiler_params=_params(1, est),
        name=name,
    )(m2, g.reshape(1, d), x2)


def _post_pre_kernel(m_ref, gpost_ref, x_ref, gpre_ref, o_ref, h_ref):
    m = m_ref[...]
    y = m * lax.rsqrt(jnp.mean(m * m, axis=-1, keepdims=True) + EPS)
    xn = x_ref[...] + y * gpost_ref[...]
    o_ref[...] = xn
    hn = xn * lax.rsqrt(jnp.mean(xn * xn, axis=-1, keepdims=True) + EPS)
    h_ref[...] = (hn * gpre_ref[...]).astype(h_ref.dtype)


def _post_pre(m2, g_post, x2, g_pre, name):
    m, d = x2.shape
    tm = _pick(m, 256, 16)
    est = 2 * (3 * _nbytes((tm, d), f32) + _nbytes((tm, d), bf16)) + 5 * _nbytes((tm, d), f32)
    row = pl.BlockSpec((tm, d), lambda i: (i, 0))
    gain = pl.BlockSpec((1, d), lambda i: (0, 0))
    return pl.pallas_call(
        _post_pre_kernel,
        grid=(m // tm,),
        in_specs=[row, gain, row, gain],
        out_specs=[row, row],
        out_shape=[jax.ShapeDtypeStruct((m, d), f32), jax.ShapeDtypeStruct((m, d), bf16)],
        compiler_params=_params(1, est),
        name=name,
    )(m2, g_post.reshape(1, d), x2, g_pre.reshape(1, d))


def _mm_kernel(x_ref, w_ref, o_ref, *, w_is_nk):
    dims = (((1,), (1,)), ((), ())) if w_is_nk else (((1,), (0,)), ((), ()))
    o_ref[...] = lax.dot_general(x_ref[...], w_ref[...], dims, preferred_element_type=f32).astype(o_ref.dtype)


def _mm(x, w3, layer, col0, n, *, tm_target, tn_target, w_single_buffer=False, w_is_nk=False,
        out_dtype=f32, name):
    m, k = x.shape
    assert w3.shape[2 if w_is_nk else 1] == k
    tm = _pick(m, tm_target, 16)
    tn = _pick(n, tn_target, V7X_LANES)
    assert col0 % tn == 0 and n % tn == 0 and m % tm == 0
    cb = col0 // tn
    w_bufs = 1 if w_single_buffer else 2
    est = (2 * _nbytes((tm, k), x.dtype) + w_bufs * _nbytes((k, tn), w3.dtype)
           + _nbytes((k, tn), bf16) + 2 * _nbytes((tm, tn), out_dtype) + _nbytes((tm, tn), f32))
    w_kwargs = dict(pipeline_mode=pl.Buffered(1)) if w_single_buffer else {}
    if w_is_nk:
        w_spec = pl.BlockSpec((None, tn, k), lambda j, i: (layer, cb + j, 0), **w_kwargs)
    else:
        w_spec = pl.BlockSpec((None, k, tn), lambda j, i: (layer, 0, cb + j), **w_kwargs)
    return pl.pallas_call(
        functools.partial(_mm_kernel, w_is_nk=w_is_nk),
        grid=(n // tn, m // tm),
        in_specs=[pl.BlockSpec((tm, k), lambda j, i: (i, 0)), w_spec],
        out_specs=pl.BlockSpec((tm, tn), lambda j, i: (i, j)),
        out_shape=jax.ShapeDtypeStruct((m, n), out_dtype),
        compiler_params=_params(2, est),
        name=name,
    )(x, w3)


def _round_kernel(w_ref, o_ref):
    o_ref[...] = w_ref[...].astype(o_ref.dtype)


def _round_bf16(w3, name):
    layers, k, n = w3.shape
    tk = _pick(k, 1024, 16)
    est = 2 * (_nbytes((tk, n), f32) + _nbytes((tk, n), bf16)) + _nbytes((tk, n), f32)
    return pl.pallas_call(
        _round_kernel,
        grid=(layers, k // tk),
        in_specs=[pl.BlockSpec((None, tk, n), lambda l, i: (l, i, 0))],
        out_specs=pl.BlockSpec((None, tk, n), lambda l, i: (l, i, 0)),
        out_shape=jax.ShapeDtypeStruct((layers, k, n), bf16),
        compiler_params=_params(2, est),
        name=name,
    )(w3)


def _mm_tail(x, w_tail_nk, name):
    m, k = x.shape
    n = w_tail_nk.shape[0]
    tm = _pick(m, 1024, 16)
    est = 2 * _nbytes((tm, k), x.dtype) + 2 * _nbytes((V7X_LANES, k), f32) + 4 * _nbytes((tm, V7X_LANES), f32)
    return pl.pallas_call(
        functools.partial(_mm_kernel, w_is_nk=True),
        grid=(m // tm,),
        in_specs=[pl.BlockSpec((tm, k), lambda i: (i, 0)),
                  pl.BlockSpec((n, k), lambda i: (0, 0))],
        out_specs=pl.BlockSpec((tm, n), lambda i: (i, 0)),
        out_shape=jax.ShapeDtypeStruct((m, n), f32),
        compiler_params=_params(1, est),
        name=name,
    )(x, w_tail_nk)


def _conv3(u, h0, h1, w_ref):
    row = lax.broadcasted_iota(jnp.int32, u.shape, 0)
    u1 = jnp.where(row == 0, h1, pltpu.roll(u, 1, 0))
    u2 = jnp.where(row == 0, h0, jnp.where(row == 1, h1, pltpu.roll(u, 2, 0)))
    return w_ref[0:1, :] * u2 + w_ref[1:2, :] * u1 + w_ref[2:3, :] * u


def _conva_kernel(*refs, has_hist):
    if has_hist:
        ab_ref, ac_ref, ah_ref, w_ref, hist_ref, y_ref, st_ref = refs
        h0, h1 = hist_ref[0, 0:1, :], hist_ref[0, 1:2, :]
    else:
        ab_ref, ac_ref, ah_ref, w_ref, y_ref, st_ref = refs
        h0 = h1 = jnp.zeros((1, ac_ref.shape[2]), f32)
    p = ac_ref[0] * ah_ref[0]
    z = _conv3(p, h0, h1, w_ref)
    y_ref[0] = (ab_ref[0] * z).astype(y_ref.dtype)
    length = p.shape[0]
    st_ref[0] = p[length - 2:length, :]


def _conv_a(proj3, a_width, w_conv, hist4, layer, name):
    b, length, _ = proj3.shape
    tc = _pick(a_width, 512, V7X_LANES)
    na = a_width // tc
    has_hist = hist4 is not None
    in_specs = [pl.BlockSpec((1, length, tc), lambda bi, j: (bi, 0, j)),
                pl.BlockSpec((1, length, tc), lambda bi, j: (bi, 0, na + j)),
                pl.BlockSpec((1, length, tc), lambda bi, j: (bi, 0, 2 * na + j)),
                pl.BlockSpec((3, tc), lambda bi, j: (0, j))]
    args = [proj3, proj3, proj3, w_conv]
    if has_hist:
        in_specs.append(pl.BlockSpec((None, 1, 2, tc), lambda bi, j: (layer, bi, 0, j)))
        args.append(hist4)
    est = 2 * 3 * _nbytes((length, tc), f32) + 2 * _nbytes((length, tc), bf16) + 6 * _nbytes((length, tc), f32)
    return pl.pallas_call(
        functools.partial(_conva_kernel, has_hist=has_hist),
        grid=(b, na),
        in_specs=in_specs,
        out_specs=[pl.BlockSpec((1, length, tc), lambda bi, j: (bi, 0, j)),
                   pl.BlockSpec((1, 2, tc), lambda bi, j: (bi, 0, j))],
        out_shape=[jax.ShapeDtypeStruct((b, length, a_width), bf16),
                   jax.ShapeDtypeStruct((b, 2, a_width), f32)],
        compiler_params=_params(2, est),
        name=name,
    )(*args)


def _ffn_act_kernel(*refs, has_hist):
    if has_hist:
        ug_ref, uv_ref, wg_ref, wv_ref, bg_ref, bv_ref, hg_ref, hv_ref, act_ref, sg_ref, sv_ref = refs
        hg0, hg1 = hg_ref[0, 0:1, :], hg_ref[0, 1:2, :]
        hv0, hv1 = hv_ref[0, 0:1, :], hv_ref[0, 1:2, :]
    else:
        ug_ref, uv_ref, wg_ref, wv_ref, bg_ref, bv_ref, act_ref, sg_ref, sv_ref = refs
        hg0 = hg1 = hv0 = hv1 = jnp.zeros((1, ug_ref.shape[2]), f32)
    ug = ug_ref[0]
    uv = uv_ref[0]
    gate = _conv3(ug, hg0, hg1, wg_ref) + bg_ref[...]
    val = _conv3(uv, hv0, hv1, wv_ref) + bv_ref[...]
    act_ref[0] = (gate * jax.nn.sigmoid(gate) * val).astype(act_ref.dtype)
    length = ug.shape[0]
    sg_ref[0] = ug[length - 2:length, :]
    sv_ref[0] = uv[length - 2:length, :]


def _ffn_act(u3, d_ff, w_conv, b_conv, hist4, layer, name):
    b, length, _ = u3.shape
    tc = _pick(d_ff, 512, V7X_LANES)
    nf = d_ff // tc
    has_hist = hist4 is not None
    blk = lambda off: pl.BlockSpec((1, length, tc), lambda bi, j: (bi, 0, off + j))
    wblk = lambda off: pl.BlockSpec((3, tc), lambda bi, j: (0, off + j))
    bblk = lambda off: pl.BlockSpec((1, tc), lambda bi, j: (0, off + j))
    in_specs = [blk(0), blk(nf), wblk(0), wblk(nf), bblk(0), bblk(nf)]
    b2 = b_conv.reshape(1, 2 * d_ff)
    args = [u3, u3, w_conv, w_conv, b2, b2]
    if has_hist:
        hblk = lambda off: pl.BlockSpec((None, 1, 2, tc), lambda bi, j: (layer, bi, 0, off + j))
        in_specs += [hblk(0), hblk(nf)]
        args += [hist4, hist4]
    est = 2 * 2 * _nbytes((length, tc), f32) + 2 * _nbytes((length, tc), bf16) + 8 * _nbytes((length, tc), f32)
    act, sg, sv = pl.pallas_call(
        functools.partial(_ffn_act_kernel, has_hist=has_hist),
        grid=(b, nf),
        in_specs=in_specs,
        out_specs=[pl.BlockSpec((1, length, tc), lambda bi, j: (bi, 0, j)),
                   pl.BlockSpec((1, 2, tc), lambda bi, j: (bi, 0, j)),
                   pl.BlockSpec((1, 2, tc), lambda bi, j: (bi, 0, j))],
        out_shape=[jax.ShapeDtypeStruct((b, length, d_ff), bf16),
                   jax.ShapeDtypeStruct((b, 2, d_ff), f32),
                   jax.ShapeDtypeStruct((b, 2, d_ff), f32)],
        compiler_params=_params(2, est),
        name=name,
    )(*args)
    return act, jnp.concatenate([sg, sv], axis=-1)


def _ffn_up_act_kernel(x_ref, wg_ref, wv_ref, cg_ref, cv_ref, bg_ref, bv_ref, act_ref, sg_ref, sv_ref,
                       carry_ref, wgb_ref, wvb_ref, *, tiles_per_seq, sub):
    i = pl.program_id(1)
    tm = x_ref.shape[0]

    @pl.when(i % tiles_per_seq == 0)
    def _():
        carry_ref[...] = jnp.zeros(carry_ref.shape, f32)

    @pl.when(i == 0)
    def _():
        wgb_ref[...] = wg_ref[...].astype(bf16)
        wvb_ref[...] = wv_ref[...].astype(bf16)

    dims = (((1,), (0,)), ((), ()))
    hg0, hg1 = carry_ref[0:1, :], carry_ref[1:2, :]
    hv0, hv1 = carry_ref[2:3, :], carry_ref[3:4, :]
    def project(r0):
        x = x_ref[r0:r0 + sub, :]
        return (lax.dot_general(x, wgb_ref[...], dims, preferred_element_type=f32),
                lax.dot_general(x, wvb_ref[...], dims, preferred_element_type=f32))

    starts = list(range(0, tm, sub))
    nxt = project(starts[0])
    for n, r0 in enumerate(starts):
        ug, uv = nxt
        if n + 1 < len(starts):
            nxt = project(starts[n + 1])
        gate = _conv3(ug, hg0, hg1, cg_ref) + bg_ref[...]
        val = _conv3(uv, hv0, hv1, cv_ref) + bv_ref[...]
        act_ref[r0:r0 + sub, :] = (gate * jax.nn.sigmoid(gate) * val).astype(act_ref.dtype)
        hg0, hg1 = ug[sub - 2:sub - 1, :], ug[sub - 1:sub, :]
        hv0, hv1 = uv[sub - 2:sub - 1, :], uv[sub - 1:sub, :]
    tail_g = jnp.concatenate([hg0, hg1], axis=0)
    tail_v = jnp.concatenate([hv0, hv1], axis=0)
    carry_ref[0:2, :] = tail_g
    carry_ref[2:4, :] = tail_v
    sg_ref[0] = tail_g
    sv_ref[0] = tail_v


def _ffn_up_act(h, b, length, w_up, layer, d_ff, w_conv, b_conv, name):
    m, k = h.shape
    tm = _pick(length, 1024, 16)
    tc = _pick(d_ff, 256, V7X_LANES)
    nf = d_ff // tc
    tiles_per_seq = length // tm
    b2 = b_conv.reshape(1, 2 * d_ff)
    est = (2 * _nbytes((tm, k), h.dtype) + 2 * 2 * _nbytes((k, tc), f32) + 2 * _nbytes((k, tc), bf16)
           + 2 * _nbytes((tm, tc), bf16) + 12 * _nbytes((tm, tc), f32))
    act, sg, sv = pl.pallas_call(
        functools.partial(_ffn_up_act_kernel, tiles_per_seq=tiles_per_seq, sub=_pick(tm, 256, 16)),
        grid=(nf, m // tm),
        in_specs=[pl.BlockSpec((tm, k), lambda j, i: (i, 0)),
                  pl.BlockSpec((None, k, tc), lambda j, i: (layer, 0, j)),
                  pl.BlockSpec((None, k, tc), lambda j, i: (layer, 0, nf + j)),
                  pl.BlockSpec((3, tc), lambda j, i: (0, j)),
                  pl.BlockSpec((3, tc), lambda j, i: (0, nf + j)),
                  pl.BlockSpec((1, tc), lambda j, i: (0, j)),
                  pl.BlockSpec((1, tc), lambda j, i: (0, nf + j))],
        out_specs=[pl.BlockSpec((tm, tc), lambda j, i: (i, j)),
                   pl.BlockSpec((1, 2, tc), lambda j, i: (i // tiles_per_seq, 0, j)),
                   pl.BlockSpec((1, 2, tc), lambda j, i: (i // tiles_per_seq, 0, j))],
        out_shape=[jax.ShapeDtypeStruct((m, d_ff), bf16),
                   jax.ShapeDtypeStruct((b, 2, d_ff), f32),
                   jax.ShapeDtypeStruct((b, 2, d_ff), f32)],
        scratch_shapes=[pltpu.VMEM((V7X_SUBLANES, tc), f32),
                        pltpu.VMEM((k, tc), bf16),
                        pltpu.VMEM((k, tc), bf16)],
        compiler_params=_params(2, est),
        name=name,
    )(h, w_up, w_up, w_conv, w_conv, b2, b2)
    return act, jnp.concatenate([sg, sv], axis=-1)


def _gla_kernel(*refs, chunk, n_chunks, scale, has_s0):
    if has_s0:
        q_ref, k_ref, v_ref, r_ref, gl_ref, wg_ref, bg_ref, gn_ref, s0_ref, y_ref, s_ref, st_ref = refs
    else:
        q_ref, k_ref, v_ref, r_ref, gl_ref, wg_ref, bg_ref, gn_ref, y_ref, s_ref, st_ref = refs
    t = pl.program_id(2)

    @pl.when(t == 0)
    def _():
        if has_s0:
            st_ref[...] = s0_ref[0, 0].T
        else:
            st_ref[...] = jnp.zeros(st_ref.shape, f32)

    ri = lax.broadcasted_iota(jnp.int32, (chunk, chunk), 0)
    ci = lax.broadcasted_iota(jnp.int32, (chunk, chunk), 1)
    causal = ri >= ci
    tri = causal.astype(f32)
    nt = (((1,), (1,)), ((), ()))
    tn = (((0,), (0,)), ((), ()))

    def body(c, carry):
        sl = pl.ds(pl.multiple_of(c * chunk, chunk), chunk)
        q = q_ref[0, sl, :] * scale
        k = k_ref[0, sl, :]
        v = v_ref[0, sl, :]
        pre = jnp.dot(gl_ref[0, sl, :], wg_ref[...], preferred_element_type=f32) + bg_ref[...]
        log_a = _log_sigmoid(pre) / GLA_TAU
        bcum = jnp.dot(tri, log_a, precision=HIGHEST, preferred_element_type=f32)
        b_last = bcum[chunk - 1:chunk, :]
        q_dec = q * jnp.exp(bcum)
        k_inv = k * jnp.exp(-bcum)
        k_tail = k * jnp.exp(b_last - bcum)
        att = lax.dot_general(q_dec, k_inv, nt, preferred_element_type=f32)
        att = jnp.where(causal, att, 0.0)
        st = st_ref[...]
        o = (lax.dot_general(q_dec, st, nt, preferred_element_type=f32)
             + jnp.dot(att, v, preferred_element_type=f32))
        st_ref[...] = jnp.exp(b_last) * st + lax.dot_general(v, k_tail, tn, preferred_element_type=f32)
        on = o * lax.rsqrt(jnp.mean(o * o, axis=-1, keepdims=True) + EPS) * gn_ref[...]
        r = r_ref[0, sl, :]
        y_ref[0, sl, :] = (on * (r * jax.nn.sigmoid(r))).astype(y_ref.dtype)
        return carry

    lax.fori_loop(0, n_chunks, body, 0)

    @pl.when(t == pl.num_programs(2) - 1)
    def _():
        s_ref[0, 0] = st_ref[...].T


def _gla(proj3, g_low3, offs, dims, w_gate_up, b_gate, g_norm, s0_5, layer, name):
    b, length, _ = proj3.shape
    heads, dk, dv = dims
    q_off, k_off, v_off, r_off = offs
    rank = g_low3.shape[-1]
    chunk = _gcd(length, GLA_CHUNK)
    tl = _pick(length, 512, chunk)
    n_chunks = tl // chunk
    assert q_off % dk == 0 and k_off % dk == 0 and v_off % dv == 0 and r_off % dv == 0
    qb, kb, vb, rb = q_off // dk, k_off // dk, v_off // dv, r_off // dv
    has_s0 = s0_5 is not None
    in_specs = [pl.BlockSpec((1, tl, dk), lambda bi, h, t: (bi, t, qb + h)),
                pl.BlockSpec((1, tl, dk), lambda bi, h, t: (bi, t, kb + h)),
                pl.BlockSpec((1, tl, dv), lambda bi, h, t: (bi, t, vb + h)),
                pl.BlockSpec((1, tl, dv), lambda bi, h, t: (bi, t, rb + h)),
                pl.BlockSpec((1, tl, rank), lambda bi, h, t: (bi, t, 0)),
                pl.BlockSpec((rank, dk), lambda bi, h, t: (0, h)),
                pl.BlockSpec((1, dk), lambda bi, h, t: (0, h)),
                pl.BlockSpec((1, dv), lambda bi, h, t: (0, h))]
    args = [proj3, proj3, proj3, proj3, g_low3, w_gate_up, b_gate.reshape(1, -1), g_norm.reshape(1, -1)]
    if has_s0:
        in_specs.append(pl.BlockSpec((None, 1, 1, dk, dv), lambda bi, h, t: (layer, bi, h, 0, 0)))
        args.append(s0_5)
    est = (2 * (2 * _nbytes((tl, dk), f32) + 2 * _nbytes((tl, dv), f32) + _nbytes((tl, V7X_LANES), f32))
           + 2 * _nbytes((tl, dv), bf16) + 5 * _nbytes((dk, dv), f32) + 16 * _nbytes((chunk, dv), f32))
    return pl.pallas_call(
        functools.partial(_gla_kernel, chunk=chunk, n_chunks=n_chunks, scale=dk ** -0.5, has_s0=has_s0),
        grid=(b, heads, length // tl),
        in_specs=in_specs,
        out_specs=[pl.BlockSpec((1, tl, dv), lambda bi, h, t: (bi, t, h)),
                   pl.BlockSpec((1, 1, dk, dv), lambda bi, h, t: (bi, h, 0, 0))],
        out_shape=[jax.ShapeDtypeStruct((b, length, heads * dv), bf16),
                   jax.ShapeDtypeStruct((b, heads, dk, dv), f32)],
        scratch_shapes=[pltpu.VMEM((dv, dk), f32)],
        compiler_params=_params(3, est),
        name=name,
    )(*args)


def _gcd(a, b):
    while b:
        a, b = b, a % b
    return a


def _logf_kernel(fl_ref, bf_ref, lf_ref, c_ref):
    lf = _log_sigmoid(fl_ref[0] + bf_ref[...])
    lf_ref[0] = lf
    c = lf
    row = lax.broadcasted_iota(jnp.int32, c.shape, 0)
    s = 1
    while s < c.shape[0]:
        c = c + jnp.where(row >= s, pltpu.roll(c, s, 0), 0.0)
        s *= 2
    c_ref[0] = c


def _logf_cumsum(fl3, b_f, name):
    b, length, h = fl3.shape
    est = 16 * _nbytes((length, V7X_LANES), f32)
    return pl.pallas_call(
        _logf_kernel,
        grid=(b,),
        in_specs=[pl.BlockSpec((1, length, h), lambda bi: (bi, 0, 0)),
                  pl.BlockSpec((1, h), lambda bi: (0, 0))],
        out_specs=[pl.BlockSpec((1, length, h), lambda bi: (bi, 0, 0)),
                   pl.BlockSpec((1, length, h), lambda bi: (bi, 0, 0))],
        out_shape=[jax.ShapeDtypeStruct((b, length, h), f32),
                   jax.ShapeDtypeStruct((b, length, h), f32)],
        compiler_params=_params(1, est),
        name=name,
    )(fl3, b_f.reshape(1, h))


def _flash_kernel(q_ref, k_ref, v_ref, c_ref, o_ref, crow_ref, *, tq, n_heads, scale):
    h = pl.program_id(1)
    qi = pl.program_id(2)
    nt = (((1,), (1,)), ((), ()))
    dh = q_ref.shape[2]

    sel_col = (lax.broadcasted_iota(jnp.int32, (n_heads, V7X_LANES), 0) == h).astype(f32)
    sel_row = (lax.broadcasted_iota(jnp.int32, (V7X_SUBLANES, n_heads), 1) == h).astype(f32)
    q0 = pl.multiple_of(qi * tq, tq)
    c_q = jnp.dot(c_ref[0, pl.ds(q0, tq), :], sel_col, precision=HIGHEST, preferred_element_type=f32)
    c_q = jnp.tile(c_q, (1, tq // V7X_LANES))

    @pl.when(qi == 0)
    def _():
        crow_ref[...] = lax.dot_general(sel_row, c_ref[0], nt, precision=HIGHEST, preferred_element_type=f32)

    q = q_ref[0] * scale

    def step(j, carry, masked):
        m, l, acc = carry
        k0 = pl.multiple_of(j * tq, tq)
        kb = k_ref[0, pl.ds(k0, tq), :]
        vb = v_ref[0, pl.ds(k0, tq), :]
        s = lax.dot_general(q, kb, nt, preferred_element_type=f32)
        s = s + (c_q - crow_ref[0:1, pl.ds(k0, tq)])
        if masked:
            ri = lax.broadcasted_iota(jnp.int32, (tq, tq), 0)
            ci = lax.broadcasted_iota(jnp.int32, (tq, tq), 1)
            s = jnp.where(ci <= ri, s, -jnp.inf)
        m_new = jnp.maximum(m, jnp.max(s, axis=-1, keepdims=True))
        p = jnp.exp(s - m_new)
        alpha = jnp.exp(m - m_new)
        l = alpha * l + jnp.sum(p, axis=-1, keepdims=True)
        acc = alpha * acc + jnp.dot(p, vb, preferred_element_type=f32)
        return m_new, l, acc

    init = (jnp.full((tq, 1), -jnp.inf, f32), jnp.zeros((tq, 1), f32), jnp.zeros((tq, dh), f32))
    carry = lax.fori_loop(0, qi, functools.partial(step, masked=False), init)
    _, l, acc = step(qi, carry, True)
    o_ref[0] = (acc / l).astype(o_ref.dtype)


def _flash(qkv3, c3, n_heads, name):
    b, length, d3 = qkv3.shape
    d = d3 // 3
    dh = d // n_heads
    tq = _pick(length, 512, V7X_LANES)
    est = (2 * _nbytes((tq, dh), f32) + 4 * _nbytes((length, dh), f32) + 2 * _nbytes((length, V7X_LANES), f32)
           + 2 * _nbytes((tq, dh), bf16) + _nbytes((V7X_SUBLANES, length), f32) + 8 * _nbytes((tq, tq), f32))
    return pl.pallas_call(
        functools.partial(_flash_kernel, tq=tq, n_heads=n_heads, scale=dh ** -0.5),
        grid=(b, n_heads, length // tq),
        in_specs=[pl.BlockSpec((1, tq, dh), lambda bi, h, qi: (bi, qi, h)),
                  pl.BlockSpec((1, length, dh), lambda bi, h, qi: (bi, 0, n_heads + h)),
                  pl.BlockSpec((1, length, dh), lambda bi, h, qi: (bi, 0, 2 * n_heads + h)),
                  pl.BlockSpec((1, length, n_heads), lambda bi, h, qi: (bi, 0, 0))],
        out_specs=pl.BlockSpec((1, tq, dh), lambda bi, h, qi: (bi, qi, h)),
        out_shape=jax.ShapeDtypeStruct((b, length, d), bf16),
        scratch_shapes=[pltpu.VMEM((V7X_SUBLANES, length), f32)],
        compiler_params=_params(3, est),
        name=name,
    )(qkv3, qkv3, qkv3, c3)


def _decode_logits_kernel(pt_ref, q_ref, kn_ref, lfs_ref, kp_ref, fp_ref, sp_ref, sn_ref,
                          qbd_ref, cnew_ref, carry_ref, *, n_heads, scale):
    del pt_ref
    p = pl.program_id(1)
    lq = q_ref.shape[1]
    d = q_ref.shape[2]
    dh = d // n_heads
    nq = n_heads * lq
    page = kp_ref.shape[0]
    expand = ((lax.broadcasted_iota(jnp.int32, (n_heads, nq), 1) // lq)
              == lax.broadcasted_iota(jnp.int32, (n_heads, nq), 0)).astype(f32)

    @pl.when(p == 0)
    def _():
        qt = jnp.tile(q_ref[0] * scale, (n_heads, 1))
        rh = lax.broadcasted_iota(jnp.int32, (nq, d), 0) // lq
        ch = lax.broadcasted_iota(jnp.int32, (nq, d), 1) // dh
        qbd = jnp.where(rh == ch, qt, 0.0).T
        qbd_ref[...] = qbd.astype(qbd_ref.dtype)
        x = jnp.dot(lfs_ref[0], expand, precision=HIGHEST, preferred_element_type=f32)
        tk = lax.broadcasted_iota(jnp.int32, (lq, nq), 0)
        tq = lax.broadcasted_iota(jnp.int32, (lq, nq), 1) % lq
        c_query = jnp.sum(jnp.where(tk <= tq, x, 0.0), axis=0, keepdims=True)
        tri = (lax.broadcasted_iota(jnp.int32, (lq, lq), 0)
               >= lax.broadcasted_iota(jnp.int32, (lq, lq), 1)).astype(f32)
        c_key = jnp.dot(tri, x, precision=HIGHEST, preferred_element_type=f32)
        cnew_ref[...] = c_query
        carry_ref[...] = jnp.zeros(carry_ref.shape, f32)
        s_new = jnp.dot(kn_ref[0].astype(qbd_ref.dtype), qbd_ref[...], preferred_element_type=f32)
        s_new = s_new + (c_query - c_key)
        sn_ref[0] = jnp.where(tk <= tq, s_new, -jnp.inf)

    fpage = fp_ref[...]
    upper = (lax.broadcasted_iota(jnp.int32, (page, page), 1)
             > lax.broadcasted_iota(jnp.int32, (page, page), 0)).astype(f32)
    suffix = jnp.dot(upper, fpage, precision=HIGHEST, preferred_element_type=f32) + carry_ref[...]
    carry_ref[...] = carry_ref[...] + jnp.sum(fpage, axis=0, keepdims=True)
    bias = jnp.dot(suffix, expand, precision=HIGHEST, preferred_element_type=f32) + cnew_ref[...]
    s = jnp.dot(kp_ref[...].astype(qbd_ref.dtype), qbd_ref[...], preferred_element_type=f32)
    sp_ref[0] = s + bias


def _decode_logits(q3, kn3, lfs3, cache_k4, cache_f4, page_table, layer, n_heads, name):
    b, lq, d = q3.shape
    n_pages = page_table.shape[1]
    page = cache_k4.shape[2]
    nq = n_heads * lq
    dh = d // n_heads
    last = n_pages - 1
    grid_spec = pltpu.PrefetchScalarGridSpec(
        num_scalar_prefetch=1,
        grid=(b, n_pages),
        in_specs=[pl.BlockSpec((1, lq, d), lambda bi, p, pt: (bi, 0, 0)),
                  pl.BlockSpec((1, lq, d), lambda bi, p, pt: (bi, 0, 0)),
                  pl.BlockSpec((1, lq, n_heads), lambda bi, p, pt: (bi, 0, 0)),
                  pl.BlockSpec((None, None, page, d), lambda bi, p, pt: (layer, pt[bi, last - p], 0, 0)),
                  pl.BlockSpec((None, None, page, n_heads), lambda bi, p, pt: (layer, pt[bi, last - p], 0, 0))],
        out_specs=[pl.BlockSpec((1, page, nq), lambda bi, p, pt: (bi, last - p, 0)),
                   pl.BlockSpec((1, lq, nq), lambda bi, p, pt: (bi, 0, 0))],
        scratch_shapes=[pltpu.VMEM((d, nq), bf16),
                        pltpu.VMEM((1, nq), f32),
                        pltpu.VMEM((1, n_heads), f32)],
    )
    est = (2 * _nbytes((page, d), f32) + _nbytes((page, d), bf16) + 3 * _nbytes((d, nq), f32)
           + 8 * _nbytes((page, nq), f32) + 6 * _nbytes((lq, d), f32))
    return pl.pallas_call(
        functools.partial(_decode_logits_kernel, n_heads=n_heads, scale=dh ** -0.5),
        grid_spec=grid_spec,
        out_shape=[jax.ShapeDtypeStruct((b, n_pages * page, nq), f32),
                   jax.ShapeDtypeStruct((b, lq, nq), f32)],
        compiler_params=_params(2, est),
        name=name,
    )(page_table, q3, kn3, lfs3, cache_k4, cache_f4)


def _decode_pv_kernel(pt_ref, sp_ref, sn_ref, vn_ref, vp_ref, o_ref, prob_ref, acc_ref, *, n_heads):
    del pt_ref
    p = pl.program_id(1)
    lq = vn_ref.shape[1]
    d = vn_ref.shape[2]
    dh = d // n_heads
    page = vp_ref.shape[0]
    tn = (((0,), (0,)), ((), ()))

    @pl.when(p == 0)
    def _():
        sp = sp_ref[0]
        sn = sn_ref[0]
        m = jnp.maximum(jnp.max(sp, axis=0, keepdims=True), jnp.max(sn, axis=0, keepdims=True))
        ep = jnp.exp(sp - m)
        en = jnp.exp(sn - m)
        inv = 1.0 / (jnp.sum(ep, axis=0, keepdims=True) + jnp.sum(en, axis=0, keepdims=True))
        prob_ref[...] = ep * inv
        acc_ref[...] = lax.dot_general(en * inv, vn_ref[0], tn, preferred_element_type=f32)

    rows = pl.ds(pl.multiple_of(p * page, page), page)
    acc_ref[...] += lax.dot_general(prob_ref[rows, :], vp_ref[...], tn, preferred_element_type=f32)

    @pl.when(p == pl.num_programs(1) - 1)
    def _():
        parts = [acc_ref[h * lq:(h + 1) * lq, h * dh:(h + 1) * dh] for h in range(n_heads)]
        o_ref[0] = jnp.concatenate(parts, axis=1).astype(o_ref.dtype)


def _decode_pv(sp, sn, vn3, cache_v4, page_table, layer, n_heads, name):
    b, lq, d = vn3.shape
    n_pages = page_table.shape[1]
    page = cache_v4.shape[2]
    past = n_pages * page
    nq = n_heads * lq
    grid_spec = pltpu.PrefetchScalarGridSpec(
        num_scalar_prefetch=1,
        grid=(b, n_pages),
        in_specs=[pl.BlockSpec((1, past, nq), lambda bi, p, pt: (bi, 0, 0)),
                  pl.BlockSpec((1, lq, nq), lambda bi, p, pt: (bi, 0, 0)),
                  pl.BlockSpec((1, lq, d), lambda bi, p, pt: (bi, 0, 0)),
                  pl.BlockSpec((None, None, page, d), lambda bi, p, pt: (layer, pt[bi, p], 0, 0))],
        out_specs=pl.BlockSpec((1, lq, d), lambda bi, p, pt: (bi, 0, 0)),
        scratch_shapes=[pltpu.VMEM((past, nq), f32),
                        pltpu.VMEM((nq, d), f32)],
    )
    est = (2 * _nbytes((past, nq), f32) + 3 * _nbytes((past, nq), f32) + 2 * _nbytes((page, d), f32)
           + _nbytes((page, d), bf16) + 3 * _nbytes((nq, d), f32))
    return pl.pallas_call(
        functools.partial(_decode_pv_kernel, n_heads=n_heads),
        grid_spec=grid_spec,
        out_shape=jax.ShapeDtypeStruct((b, lq, d), bf16),
        compiler_params=_params(2, est),
        name=name,
    )(page_table, sp, sn, vn3, cache_v4)


def _decode_kernel(pt_ref, q_ref, kn_ref, vn_ref, lfs_ref, kp_ref, vp_ref, fp_ref, o_ref,
                   qs_ref, m_ref, l_ref, acc_ref, carry_ref, *, scale):
    del pt_ref
    p = pl.program_id(1)
    lq, n_heads, dh = q_ref.shape[1], q_ref.shape[2], q_ref.shape[3]
    gh = V7X_SUBLANES
    n_groups = n_heads // gh
    nq = lq * gh
    nt = (((1,), (1,)), ((), ()))
    tn = (((0,), (0,)), ((), ()))
    s_i = lax.broadcasted_iota(jnp.int32, (gh, V7X_LANES), 0)
    c_i = lax.broadcasted_iota(jnp.int32, (gh, V7X_LANES), 1)
    diag = ((c_i % gh) == s_i) & (c_i < nq)

    def attend(k3, v3, f2, causal):
        nk = f2.shape[0]
        upper = (lax.broadcasted_iota(jnp.int32, (nk, nk), 1)
                 > lax.broadcasted_iota(jnp.int32, (nk, nk), 0)).astype(f32)
        beta = jnp.dot(upper, f2, precision=HIGHEST, preferred_element_type=f32) + carry_ref[...]
        carry_ref[...] = carry_ref[...] + jnp.sum(f2, axis=0, keepdims=True)
        keep = diag[None]
        if causal:
            key_i = lax.broadcasted_iota(jnp.int32, (nk, gh, V7X_LANES), 0)
            t_i = lax.broadcasted_iota(jnp.int32, (nk, gh, V7X_LANES), 2) // gh
            keep = keep & (key_i <= t_i)
        for a in range(n_groups):
            hs = slice(a * gh, (a + 1) * gh)
            ka = k3[:, hs, :].reshape(nk * gh, dh)
            va = v3[:, hs, :].reshape(nk * gh, dh)
            g = lax.dot_general(ka, qs_ref[a], nt, preferred_element_type=f32)
            pick = ((lax.broadcasted_iota(jnp.int32, (n_heads, V7X_LANES), 0)
                     == a * gh + lax.broadcasted_iota(jnp.int32, (n_heads, V7X_LANES), 1) % gh)
                    & (lax.broadcasted_iota(jnp.int32, (n_heads, V7X_LANES), 1) < nq)).astype(f32)
            xa = jnp.dot(beta, pick, precision=HIGHEST, preferred_element_type=f32)
            g3 = g.reshape(nk, gh, V7X_LANES) + xa[:, None, :]
            g3 = jnp.where(keep, g3, -jnp.inf)
            m_old = m_ref[a]
            m_new = jnp.maximum(m_old, jnp.max(g3, axis=(0, 1)).reshape(1, V7X_LANES))
            m_safe = jnp.where(m_new == -jnp.inf, 0.0, m_new)
            alpha = jnp.exp(m_old - m_safe)
            p3 = jnp.exp(g3 - m_safe[None])
            l_ref[a] = alpha * l_ref[a] + jnp.sum(p3, axis=(0, 1)).reshape(1, V7X_LANES)
            acc_ref[a] = alpha * acc_ref[a] + lax.dot_general(
                va, p3.reshape(nk * gh, V7X_LANES), tn, preferred_element_type=f32)
            m_ref[a] = m_new

    @pl.when(p == 0)
    def _():
        for a in range(n_groups):
            qa = q_ref[0, :, a * gh:(a + 1) * gh, :].reshape(nq, dh) * scale
            qs_ref[a] = jnp.concatenate([qa, jnp.zeros((V7X_LANES - nq, dh), f32)], axis=0)
        m_ref[...] = jnp.full(m_ref.shape, -jnp.inf, f32)
        l_ref[...] = jnp.zeros(l_ref.shape, f32)
        acc_ref[...] = jnp.zeros(acc_ref.shape, f32)
        carry_ref[...] = jnp.zeros(carry_ref.shape, f32)
        attend(kn_ref[0], vn_ref[0], lfs_ref[0], True)

    @pl.when(p > 0)
    def _():
        attend(kp_ref, vp_ref, fp_ref[...], False)

    @pl.when(p == pl.num_programs(1) - 1)
    def _():
        for a in range(n_groups):
            out_t = acc_ref[a] / l_ref[a]
            out = out_t.T
            for t in range(lq):
                o_ref[0, t, a * gh:(a + 1) * gh, :] = out[t * gh:(t + 1) * gh, :]


def _decode_attention(q4, kn4, vn4, lfs3, cache_k, cache_v, cache_f, page_table, layer, name):
    b, lq, n_heads, dh = q4.shape
    n_pages = page_table.shape[1]
    page = cache_k.shape[2]
    assert n_heads % V7X_SUBLANES == 0 and lq * V7X_SUBLANES <= V7X_LANES and dh == V7X_LANES
    n_groups = n_heads // V7X_SUBLANES
    last = n_pages - 1
    page_of = lambda bi, p, pt: pt[bi, last - jnp.maximum(p - 1, 0)]
    new_blk = lambda bi, p, pt: (bi, 0, 0, 0)
    grid_spec = pltpu.PrefetchScalarGridSpec(
        num_scalar_prefetch=1,
        grid=(b, n_pages + 1),
        in_specs=[pl.BlockSpec((1, lq, n_heads, dh), new_blk),
                  pl.BlockSpec((1, lq, n_heads, dh), new_blk),
                  pl.BlockSpec((1, lq, n_heads, dh), new_blk),
                  pl.BlockSpec((1, lq, n_heads), lambda bi, p, pt: (bi, 0, 0)),
                  pl.BlockSpec((None, None, page, n_heads, dh),
                               lambda bi, p, pt: (layer, page_of(bi, p, pt), 0, 0, 0)),
                  pl.BlockSpec((None, None, page, n_heads, dh),
                               lambda bi, p, pt: (layer, page_of(bi, p, pt), 0, 0, 0)),
                  pl.BlockSpec((None, None, page, n_heads),
                               lambda bi, p, pt: (layer, page_of(bi, p, pt), 0, 0))],
        out_specs=pl.BlockSpec((1, lq, n_heads, dh), new_blk),
        scratch_shapes=[pltpu.VMEM((n_groups, V7X_LANES, dh), f32),
                        pltpu.VMEM((n_groups, 1, V7X_LANES), f32),
                        pltpu.VMEM((n_groups, 1, V7X_LANES), f32),
                        pltpu.VMEM((n_groups, dh, V7X_LANES), f32),
                        pltpu.VMEM((1, n_heads), f32)],
    )
    est = (4 * _nbytes((page, n_heads, dh), f32) + 8 * _nbytes((lq, n_heads, dh), f32)
           + 8 * _nbytes((page * V7X_SUBLANES, V7X_LANES), f32) + 4 * _nbytes((n_groups, V7X_LANES, dh), f32))
    return pl.pallas_call(
        functools.partial(_decode_kernel, scale=dh ** -0.5),
        grid_spec=grid_spec,
        out_shape=jax.ShapeDtypeStruct((b, lq, n_heads, dh), f32),
        compiler_params=_params(2, est),
        name=name,
    )(page_table, q4, kn4, vn4, lfs3, cache_k, cache_v, cache_f)


def _even_mixer(h, b, length, e, tag, heads, w_in_even, w_conv_a, w_gate_up, b_gate, g_gla_norm, w_out_even,
                state_conv_a, state_gla):
    a_width = w_conv_a.shape[-1]
    kdim = w_gate_up.shape[-1]
    rank = w_gate_up.shape[1]
    vdim = g_gla_norm.shape[-1]
    dk, dv = kdim // heads, vdim // heads
    wide = 3 * a_width + 2 * kdim + 2 * vdim
    proj = _mm(h, w_in_even, e, 0, wide, tm_target=1024, tn_target=512, w_is_nk=True, name=f"in_even_{tag}")
    g_low = _mm_tail(h, w_in_even[e, wide:wide + rank, :], name=f"in_even_gate_{tag}")
    proj3 = proj.reshape(b, length, wide)
    y_a, conv_state = _conv_a(proj3, a_width, w_conv_a[e], state_conv_a, e, name=f"conv_a_{tag}")
    q_off = 3 * a_width
    offs = (q_off, q_off + kdim, q_off + 2 * kdim, q_off + 2 * kdim + vdim)
    y_b, s_new = _gla(proj3, g_low.reshape(b, length, rank), offs, (heads, dk, dv),
                      w_gate_up[e], b_gate[e], g_gla_norm[e], state_gla, e, name=f"gla_{tag}")
    y = jnp.concatenate([y_a, y_b], axis=-1).reshape(b * length, a_width + vdim)
    mix = _mm(y, w_out_even, e, 0, w_out_even.shape[-1], tm_target=1024, tn_target=512,
              name=f"out_even_{tag}")
    return mix, conv_state, s_new


def _fox_qkv(h, b, length, o, tag, w_in_odd, b_forget):
    d = h.shape[1]
    n_heads = b_forget.shape[-1]
    qkv = _mm(h, w_in_odd, o, 0, 3 * d, tm_target=1024, tn_target=512, w_is_nk=True, name=f"fox_qkv_{tag}")
    fl = _mm_tail(h, w_in_odd[o, 3 * d:3 * d + n_heads, :], name=f"fox_f_{tag}")
    logf, csum = _logf_cumsum(fl.reshape(b, length, n_heads), b_forget[o], name=f"fox_logf_{tag}")
    return qkv.reshape(b, length, 3 * d), logf, csum


def _ffn(h, b, length, l, tag, w_up, w_ffn_conv, b_ffn_conv, w_down, state_ffn_conv):
    d_ff = w_down.shape[1]
    if state_ffn_conv is None:
        act, hist = _ffn_up_act(h, b, length, w_up, l, d_ff, w_ffn_conv[l], b_ffn_conv[l],
                                name=f"ffn_up_{tag}")
    else:
        u = _mm(h, w_up, l, 0, 2 * d_ff, tm_target=1024, tn_target=512, name=f"ffn_up_{tag}")
        act, hist = _ffn_act(u.reshape(b, length, 2 * d_ff), d_ff, w_ffn_conv[l], b_ffn_conv[l],
                             state_ffn_conv, l, name=f"ffn_act_{tag}")
        act = act.reshape(b * length, d_ff)
    down = _mm(act, w_down, l, 0, w_down.shape[-1], tm_target=512, tn_target=512, name=f"ffn_down_{tag}")
    return down, hist


def kernel(x_prompt, x_sample, state_conv_a, state_gla, cache_k, cache_v, cache_logf, state_ffn_conv, page_table, g_mix_pre, g_mix_post, g_ffn_pre, g_ffn_post, w_in_even, w_conv_a, w_gate_up, b_gate, g_gla_norm, w_out_even, w_in_odd, b_forget, w_out_odd, w_up, w_ffn_conv, b_ffn_conv, w_down):
    bp, lp, d = x_prompt.shape
    bs, ls, _ = x_sample.shape
    depth = g_mix_pre.shape[0]
    n_heads = cache_k.shape[3]
    gla_heads = state_gla.shape[2]

    xp = x_prompt.reshape(bp * lp, d)
    xs = x_sample.reshape(bs * ls, d)
    ca_p, ca_s, sg_p, sg_s = [], [], [], []
    kp_l, vp_l, lp_l, ks_l, vs_l, ls_l = [], [], [], [], [], []
    fc_p, fc_s = [], []
    w_in_even_nk = jnp.swapaxes(w_in_even, 1, 2)
    w_in_odd_nk = jnp.swapaxes(w_in_odd, 1, 2)
    even_w = (w_in_even_nk, w_conv_a, w_gate_up, b_gate, g_gla_norm, w_out_even)
    ffn_w = (w_up, w_ffn_conv, b_ffn_conv, _round_bf16(w_down, name="round_w_down"))
    hp = _rms_cast(xp, g_mix_pre[0], name="mix_pre_p0")
    hs = _rms_cast(xs, g_mix_pre[0], name="mix_pre_s0")
    for l in range(depth):
        if l % 2 == 0:
            e = l // 2
            mp, c_p, s_p = _even_mixer(hp, bp, lp, e, f"p{l}", gla_heads, *even_w, None, None)
            ms, c_s, s_s = _even_mixer(hs, bs, ls, e, f"s{l}", gla_heads, *even_w, state_conv_a, state_gla)
            ca_p.append(c_p)
            ca_s.append(c_s)
            sg_p.append(s_p)
            sg_s.append(s_s)
        else:
            o = l // 2
            dh = d // n_heads
            qkv_p, lfp, cp = _fox_qkv(hp, bp, lp, o, f"p{l}", w_in_odd_nk, b_forget)
            ap = _flash(qkv_p, cp, n_heads, name=f"fox_attn_p{l}")
            mp = _mm(ap.reshape(bp * lp, d), w_out_odd, o, 0, d, tm_target=1024, tn_target=512,
                     name=f"fox_out_p{l}")
            qkv_s, lfs, _ = _fox_qkv(hs, bs, ls, o, f"s{l}", w_in_odd_nk, b_forget)
            qs, kn, vn = (qkv_s[:, :, c * d:(c + 1) * d].reshape(bs, ls, n_heads, dh) for c in range(3))
            a_s = _decode_attention(qs, kn, vn, lfs, cache_k, cache_v, cache_logf, page_table, o,
                                    name=f"fox_attn_s{l}")
            ms = _mm(a_s.reshape(bs * ls, d).astype(bf16), w_out_odd, o, 0, d, tm_target=1024, tn_target=512,
                     name=f"fox_out_s{l}")
            kp_l.append(qkv_p[:, :, d:2 * d].reshape(bp, lp, n_heads, dh))
            vp_l.append(qkv_p[:, :, 2 * d:].reshape(bp, lp, n_heads, dh))
            lp_l.append(lfp)
            ks_l.append(kn)
            vs_l.append(vn)
            ls_l.append(lfs)
        xp, hp = _post_pre(mp, g_mix_post[l], xp, g_ffn_pre[l], name=f"mix_post_p{l}")
        xs, hs = _post_pre(ms, g_mix_post[l], xs, g_ffn_pre[l], name=f"mix_post_s{l}")
        fp, h_p = _ffn(hp, bp, lp, l, f"p{l}", *ffn_w, None)
        fs, h_s = _ffn(hs, bs, ls, l, f"s{l}", *ffn_w, state_ffn_conv)
        fc_p.append(h_p)
        fc_s.append(h_s)
        if l + 1 < depth:
            xp, hp = _post_pre(fp, g_ffn_post[l], xp, g_mix_pre[l + 1], name=f"ffn_post_p{l}")
            xs, hs = _post_pre(fs, g_ffn_post[l], xs, g_mix_pre[l + 1], name=f"ffn_post_s{l}")
        else:
            xp = _post_residual(fp, g_ffn_post[l], xp, name=f"ffn_post_p{l}")
            xs = _post_residual(fs, g_ffn_post[l], xs, name=f"ffn_post_s{l}")

    return (xp.reshape(bp, lp, d), xs.reshape(bs, ls, d),
            jnp.stack(ca_p), jnp.stack(ca_s), jnp.stack(sg_p), jnp.stack(sg_s),
            jnp.stack(kp_l), jnp.stack(vp_l), jnp.stack(lp_l),
            jnp.stack(ks_l), jnp.stack(vs_l), jnp.stack(ls_l),
            jnp.stack(fc_p), jnp.stack(fc_s))
```

```python
import functools

import jax
import jax.numpy as jnp
from jax import lax
from jax.experimental import pallas as pl
from jax.experimental.pallas import tpu as pltpu

f32 = jnp.float32
bf16 = jnp.bfloat16

EPS = 1e-6
GLA_TAU = 16.0
GLA_CHUNK = 64
HIGHEST = lax.Precision.HIGHEST

V7X_LANES = 128
V7X_SUBLANES = 8
V7X_VMEM_BYTES = 64 * 1024 * 1024
VMEM_CAP = V7X_VMEM_BYTES - 6 * 1024 * 1024


def _pick(n, target, align):
    t = (min(target, n) // align) * align
    while t >= align:
        if n % t == 0:
            return t
        t -= align
    return n


def _params(n_axes, vmem_estimate):
    limit = min(VMEM_CAP, max(32 * 1024 * 1024, int(vmem_estimate * 1.25)))
    return pltpu.CompilerParams(dimension_semantics=("arbitrary",) * n_axes,
                                vmem_limit_bytes=limit)


def _nbytes(shape, dtype):
    n = 1
    for s in shape:
        n *= s
    return n * jnp.dtype(dtype).itemsize


def _log_sigmoid(x):
    return jnp.minimum(x, 0.0) - jnp.log1p(jnp.exp(-jnp.abs(x)))


def _rms_kernel(x_ref, g_ref, o_ref):
    x = x_ref[...]
    y = x * lax.rsqrt(jnp.mean(x * x, axis=-1, keepdims=True) + EPS)
    o_ref[...] = (y * g_ref[...]).astype(o_ref.dtype)


def _rms_cast(x2, g, name):
    m, d = x2.shape
    tm = _pick(m, 256, 16)
    est = 2 * (_nbytes((tm, d), f32) + _nbytes((tm, d), bf16)) + 4 * _nbytes((tm, d), f32)
    return pl.pallas_call(
        _rms_kernel,
        grid=(m // tm,),
        in_specs=[pl.BlockSpec((tm, d), lambda i: (i, 0)),
                  pl.BlockSpec((1, d), lambda i: (0, 0))],
        out_specs=pl.BlockSpec((tm, d), lambda i: (i, 0)),
        out_shape=jax.ShapeDtypeStruct((m, d), bf16),
        compiler_params=_params(1, est),
        name=name,
    )(x2, g.reshape(1, d))


def _post_kernel(m_ref, g_ref, x_ref, o_ref):
    m = m_ref[...]
    y = m * lax.rsqrt(jnp.mean(m * m, axis=-1, keepdims=True) + EPS)
    o_ref[...] = x_ref[...] + y * g_ref[...]


def _post_residual(m2, g, x2, name):
    m, d = x2.shape
    tm = _pick(m, 256, 8)
    est = 2 * 3 * _nbytes((tm, d), f32) + 4 * _nbytes((tm, d), f32)
    return pl.pallas_call(
        _post_kernel,
        grid=(m // tm,),
        in_specs=[pl.BlockSpec((tm, d), lambda i: (i, 0)),
                  pl.BlockSpec((1, d), lambda i: (0, 0)),
                  pl.BlockSpec((tm, d), lambda i: (i, 0))],
        out_specs=pl.BlockSpec((tm, d), lambda i: (i, 0)),
        out_shape=jax.ShapeDtypeStruct((m, d), f32),
        compiler_params=_params(1, est),
        name=name,
    )(m2, g.reshape(1, d), x2)


def _post_pre_kernel(m_ref, gpost_ref, x_ref, gpre_ref, o_ref, h_ref):
    m = m_ref[...]
    y = m * lax.rsqrt(jnp.mean(m * m, axis=-1, keepdims=True) + EPS)
    xn = x_ref[...] + y * gpost_ref[...]
    o_ref[...] = xn
    hn = xn * lax.rsqrt(jnp.mean(xn * xn, axis=-1, keepdims=True) + EPS)
    h_ref[...] = (hn * gpre_ref[...]).astype(h_ref.dtype)


def _post_pre(m2, g_post, x2, g_pre, name):
    m, d = x2.shape
    tm = _pick(m, 256, 16)
    est = 2 * (3 * _nbytes((tm, d), f32) + _nbytes((tm, d), bf16)) + 5 * _nbytes((tm, d), f32)
    row = pl.BlockSpec((tm, d), lambda i: (i, 0))
    gain = pl.BlockSpec((1, d), lambda i: (0, 0))
    return pl.pallas_call(
        _post_pre_kernel,
        grid=(m // tm,),
        in_specs=[row, gain, row, gain],
        out_specs=[row, row],
        out_shape=[jax.ShapeDtypeStruct((m, d), f32), jax.ShapeDtypeStruct((m, d), bf16)],
        compiler_params=_params(1, est),
        name=name,
    )(m2, g_post.reshape(1, d), x2, g_pre.reshape(1, d))


def _mm_kernel(x_ref, w_ref, o_ref, *, w_is_nk):
    dims = (((1,), (1,)), ((), ())) if w_is_nk else (((1,), (0,)), ((), ()))
    o_ref[...] = lax.dot_general(x_ref[...], w_ref[...], dims, preferred_element_type=f32).astype(o_ref.dtype)


def _mm(x, w3, layer, col0, n, *, tm_target, tn_target, w_single_buffer=False, w_is_nk=False,
        out_dtype=f32, name):
    m, k = x.shape
    assert w3.shape[2 if w_is_nk else 1] == k
    tm = _pick(m, tm_target, 16)
    tn = _pick(n, tn_target, V7X_LANES)
    assert col0 % tn == 0 and n % tn == 0 and m % tm == 0
    cb = col0 // tn
    w_bufs = 1 if w_single_buffer else 2
    est = (2 * _nbytes((tm, k), x.dtype) + w_bufs * _nbytes((k, tn), w3.dtype)
           + _nbytes((k, tn), bf16) + 2 * _nbytes((tm, tn), out_dtype) + _nbytes((tm, tn), f32))
    w_kwargs = dict(pipeline_mode=pl.Buffered(1)) if w_single_buffer else {}
    if w_is_nk:
        w_spec = pl.BlockSpec((None, tn, k), lambda j, i: (layer, cb + j, 0), **w_kwargs)
    else:
        w_spec = pl.BlockSpec((None, k, tn), lambda j, i: (layer, 0, cb + j), **w_kwargs)
    return pl.pallas_call(
        functools.partial(_mm_kernel, w_is_nk=w_is_nk),
        grid=(n // tn, m // tm),
        in_specs=[pl.BlockSpec((tm, k), lambda j, i: (i, 0)), w_spec],
        out_specs=pl.BlockSpec((tm, tn), lambda j, i: (i, j)),
        out_shape=jax.ShapeDtypeStruct((m, n), out_dtype),
        compiler_params=_params(2, est),
        name=name,
    )(x, w3)


def _round_kernel(w_ref, o_ref):
    o_ref[...] = w_ref[...].astype(o_ref.dtype)


def _round_bf16(w3, name):
    layers, k, n = w3.shape
    tk = _pick(k, 1024, 16)
    est = 2 * (_nbytes((tk, n), f32) + _nbytes((tk, n), bf16)) + _nbytes((tk, n), f32)
    return pl.pallas_call(
        _round_kernel,
        grid=(layers, k // tk),
        in_specs=[pl.BlockSpec((None, tk, n), lambda l, i: (l, i, 0))],
        out_specs=pl.BlockSpec((None, tk, n), lambda l, i: (l, i, 0)),
        out_shape=jax.ShapeDtypeStruct((layers, k, n), bf16),
        compiler_params=_params(2, est),
        name=name,
    )(w3)


def _mm_tail(x, w_tail_nk, name):
    m, k = x.shape
    n = w_tail_nk.shape[0]
    tm = _pick(m, 1024, 16)
    est = 2 * _nbytes((tm, k), x.dtype) + 2 * _nbytes((V7X_LANES, k), f32) + 4 * _nbytes((tm, V7X_LANES), f32)
    return pl.pallas_call(
        functools.partial(_mm_kernel, w_is_nk=True),
        grid=(m // tm,),
        in_specs=[pl.BlockSpec((tm, k), lambda i: (i, 0)),
                  pl.BlockSpec((n, k), lambda i: (0, 0))],
        out_specs=pl.BlockSpec((tm, n), lambda i: (i, 0)),
        out_shape=jax.ShapeDtypeStruct((m, n), f32),
        compiler_params=_params(1, est),
        name=name,
    )(x, w_tail_nk)


def _conv3(u, h0, h1, w_ref):
    row = lax.broadcasted_iota(jnp.int32, u.shape, 0)
    u1 = jnp.where(row == 0, h1, pltpu.roll(u, 1, 0))
    u2 = jnp.where(row == 0, h0, jnp.where(row == 1, h1, pltpu.roll(u, 2, 0)))
    return w_ref[0:1, :] * u2 + w_ref[1:2, :] * u1 + w_ref[2:3, :] * u


def _conva_kernel(*refs, has_hist):
    if has_hist:
        ab_ref, ac_ref, ah_ref, w_ref, hist_ref, y_ref, st_ref = refs
        h0, h1 = hist_ref[0, 0:1, :], hist_ref[0, 1:2, :]
    else:
        ab_ref, ac_ref, ah_ref, w_ref, y_ref, st_ref = refs
        h0 = h1 = jnp.zeros((1, ac_ref.shape[2]), f32)
    p = ac_ref[0] * ah_ref[0]
    z = _conv3(p, h0, h1, w_ref)
    y_ref[0] = (ab_ref[0] * z).astype(y_ref.dtype)
    length = p.shape[0]
    st_ref[0] = p[length - 2:length, :]


def _conv_a(proj3, a_width, w_conv, hist4, layer, name):
    b, length, _ = proj3.shape
    tc = _pick(a_width, 512, V7X_LANES)
    na = a_width // tc
    has_hist = hist4 is not None
    in_specs = [pl.BlockSpec((1, length, tc), lambda bi, j: (bi, 0, j)),
                pl.BlockSpec((1, length, tc), lambda bi, j: (bi, 0, na + j)),
                pl.BlockSpec((1, length, tc), lambda bi, j: (bi, 0, 2 * na + j)),
                pl.BlockSpec((3, tc), lambda bi, j: (0, j))]
    args = [proj3, proj3, proj3, w_conv]
    if has_hist:
        in_specs.append(pl.BlockSpec((None, 1, 2, tc), lambda bi, j: (layer, bi, 0, j)))
        args.append(hist4)
    est = 2 * 3 * _nbytes((length, tc), f32) + 2 * _nbytes((length, tc), bf16) + 6 * _nbytes((length, tc), f32)
    return pl.pallas_call(
        functools.partial(_conva_kernel, has_hist=has_hist),
        grid=(b, na),
        in_specs=in_specs,
        out_specs=[pl.BlockSpec((1, length, tc), lambda bi, j: (bi, 0, j)),
                   pl.BlockSpec((1, 2, tc), lambda bi, j: (bi, 0, j))],
        out_shape=[jax.ShapeDtypeStruct((b, length, a_width), bf16),
                   jax.ShapeDtypeStruct((b, 2, a_width), f32)],
        compiler_params=_params(2, est),
        name=name,
    )(*args)


def _ffn_act_kernel(*refs, has_hist):
    if has_hist:
        ug_ref, uv_ref, wg_ref, wv_ref, bg_ref, bv_ref, hg_ref, hv_ref, act_ref, sg_ref, sv_ref = refs
        hg0, hg1 = hg_ref[0, 0:1, :], hg_ref[0, 1:2, :]
        hv0, hv1 = hv_ref[0, 0:1, :], hv_ref[0, 1:2, :]
    else:
        ug_ref, uv_ref, wg_ref, wv_ref, bg_ref, bv_ref, act_ref, sg_ref, sv_ref = refs
        hg0 = hg1 = hv0 = hv1 = jnp.zeros((1, ug_ref.shape[2]), f32)
    ug = ug_ref[0]
    uv = uv_ref[0]
    gate = _conv3(ug, hg0, hg1, wg_ref) + bg_ref[...]
    val = _conv3(uv, hv0, hv1, wv_ref) + bv_ref[...]
    act_ref[0] = (gate * jax.nn.sigmoid(gate) * val).astype(act_ref.dtype)
    length = ug.shape[0]
    sg_ref[0] = ug[length - 2:length, :]
    sv_ref[0] = uv[length - 2:length, :]


def _ffn_act(u3, d_ff, w_conv, b_conv, hist4, layer, name):
    b, length, _ = u3.shape
    tc = _pick(d_ff, 512, V7X_LANES)
    nf = d_ff // tc
    has_hist = hist4 is not None
    blk = lambda off: pl.BlockSpec((1, length, tc), lambda bi, j: (bi, 0, off + j))
    wblk = lambda off: pl.BlockSpec((3, tc), lambda bi, j: (0, off + j))
    bblk = lambda off: pl.BlockSpec((1, tc), lambda bi, j: (0, off + j))
    in_specs = [blk(0), blk(nf), wblk(0), wblk(nf), bblk(0), bblk(nf)]
    b2 = b_conv.reshape(1, 2 * d_ff)
    args = [u3, u3, w_conv, w_conv, b2, b2]
    if has_hist:
        hblk = lambda off: pl.BlockSpec((None, 1, 2, tc), lambda bi, j: (layer, bi, 0, off + j))
        in_specs += [hblk(0), hblk(nf)]
        args += [hist4, hist4]
    est = 2 * 2 * _nbytes((length, tc), f32) + 2 * _nbytes((length, tc), bf16) + 8 * _nbytes((length, tc), f32)
    act, sg, sv = pl.pallas_call(
        functools.partial(_ffn_act_kernel, has_hist=has_hist),
        grid=(b, nf),
        in_specs=in_specs,
        out_specs=[pl.BlockSpec((1, length, tc), lambda bi, j: (bi, 0, j)),
                   pl.BlockSpec((1, 2, tc), lambda bi, j: (bi, 0, j)),
                   pl.BlockSpec((1, 2, tc), lambda bi, j: (bi, 0, j))],
        out_shape=[jax.ShapeDtypeStruct((b, length, d_ff), bf16),
                   jax.ShapeDtypeStruct((b, 2, d_ff), f32),
                   jax.ShapeDtypeStruct((b, 2, d_ff), f32)],
        compiler_params=_params(2, est),
        name=name,
    )(*args)
    return act, jnp.concatenate([sg, sv], axis=-1)


def _ffn_up_act_kernel(x_ref, wg_ref, wv_ref, cg_ref, cv_ref, bg_ref, bv_ref, act_ref, sg_ref, sv_ref,
                       carry_ref, *, tiles_per_seq):
    i = pl.program_id(1)

    @pl.when(i % tiles_per_seq == 0)
    def _():
        carry_ref[...] = jnp.zeros(carry_ref.shape, f32)

    x = x_ref[...]
    dims = (((1,), (0,)), ((), ()))
    ug = lax.dot_general(x, wg_ref[...], dims, preferred_element_type=f32)
    uv = lax.dot_general(x, wv_ref[...], dims, preferred_element_type=f32)
    gate = _conv3(ug, carry_ref[0:1, :], carry_ref[1:2, :], cg_ref) + bg_ref[...]
    val = _conv3(uv, carry_ref[2:3, :], carry_ref[3:4, :], cv_ref) + bv_ref[...]
    act_ref[...] = (gate * jax.nn.sigmoid(gate) * val).astype(act_ref.dtype)
    tm = ug.shape[0]
    carry_ref[0:2, :] = ug[tm - 2:tm, :]
    carry_ref[2:4, :] = uv[tm - 2:tm, :]
    sg_ref[0] = ug[tm - 2:tm, :]
    sv_ref[0] = uv[tm - 2:tm, :]


def _ffn_up_act(h, b, length, w_up, layer, d_ff, w_conv, b_conv, name):
    m, k = h.shape
    tm = _pick(length, 1024, 16)
    tc = _pick(d_ff, 256, V7X_LANES)
    nf = d_ff // tc
    tiles_per_seq = length // tm
    b2 = b_conv.reshape(1, 2 * d_ff)
    est = (2 * _nbytes((tm, k), h.dtype) + 2 * 2 * _nbytes((k, tc), f32) + 2 * _nbytes((k, tc), bf16)
           + 2 * _nbytes((tm, tc), bf16) + 12 * _nbytes((tm, tc), f32))
    act, sg, sv = pl.pallas_call(
        functools.partial(_ffn_up_act_kernel, tiles_per_seq=tiles_per_seq),
        grid=(nf, m // tm),
        in_specs=[pl.BlockSpec((tm, k), lambda j, i: (i, 0)),
                  pl.BlockSpec((None, k, tc), lambda j, i: (layer, 0, j)),
                  pl.BlockSpec((None, k, tc), lambda j, i: (layer, 0, nf + j)),
                  pl.BlockSpec((3, tc), lambda j, i: (0, j)),
                  pl.BlockSpec((3, tc), lambda j, i: (0, nf + j)),
                  pl.BlockSpec((1, tc), lambda j, i: (0, j)),
                  pl.BlockSpec((1, tc), lambda j, i: (0, nf + j))],
        out_specs=[pl.BlockSpec((tm, tc), lambda j, i: (i, j)),
                   pl.BlockSpec((1, 2, tc), lambda j, i: (i // tiles_per_seq, 0, j)),
                   pl.BlockSpec((1, 2, tc), lambda j, i: (i // tiles_per_seq, 0, j))],
        out_shape=[jax.ShapeDtypeStruct((m, d_ff), bf16),
                   jax.ShapeDtypeStruct((b, 2, d_ff), f32),
                   jax.ShapeDtypeStruct((b, 2, d_ff), f32)],
        scratch_shapes=[pltpu.VMEM((V7X_SUBLANES, tc), f32)],
        compiler_params=_params(2, est),
        name=name,
    )(h, w_up, w_up, w_conv, w_conv, b2, b2)
    return act, jnp.concatenate([sg, sv], axis=-1)


def _gla_kernel(*refs, chunk, n_chunks, heads, scale, has_s0):
    if has_s0:
        q_ref, k_ref, v_ref, r_ref, gl_ref, wg_ref, bg_ref, gn_ref, s0_ref, y_ref, s_ref, st_ref = refs
    else:
        q_ref, k_ref, v_ref, r_ref, gl_ref, wg_ref, bg_ref, gn_ref, y_ref, s_ref, st_ref = refs
    t = pl.program_id(1)
    dk = q_ref.shape[2] // heads
    dv = v_ref.shape[2] // heads

    @pl.when(t == 0)
    def _():
        for h in range(heads):
            if has_s0:
                st_ref[h] = s0_ref[0, h].T
            else:
                st_ref[h] = jnp.zeros(st_ref.shape[1:], f32)

    ri = lax.broadcasted_iota(jnp.int32, (chunk, chunk), 0)
    ci = lax.broadcasted_iota(jnp.int32, (chunk, chunk), 1)
    causal = ri >= ci
    tri = causal.astype(f32)
    nt = (((1,), (1,)), ((), ()))
    tn = (((0,), (0,)), ((), ()))

    def body(c, carry):
        sl = pl.ds(pl.multiple_of(c * chunk, chunk), chunk)
        pre = jnp.dot(gl_ref[0, sl, :], wg_ref[...], preferred_element_type=f32) + bg_ref[...]
        log_a = _log_sigmoid(pre) / GLA_TAU
        bcum_all = jnp.dot(tri, log_a, precision=HIGHEST, preferred_element_type=f32)
        for h in range(heads):
            ks = slice(h * dk, (h + 1) * dk)
            vs = slice(h * dv, (h + 1) * dv)
            bcum = bcum_all[:, ks]
            q = q_ref[0, sl, ks] * scale
            k = k_ref[0, sl, ks]
            v = v_ref[0, sl, vs]
            b_last = bcum[chunk - 1:chunk, :]
            q_dec = q * jnp.exp(bcum)
            k_inv = k * jnp.exp(-bcum)
            k_tail = k * jnp.exp(b_last - bcum)
            att = lax.dot_general(q_dec, k_inv, nt, preferred_element_type=f32)
            att = jnp.where(causal, att, 0.0)
            st = st_ref[h]
            o = (lax.dot_general(q_dec, st, nt, preferred_element_type=f32)
                 + jnp.dot(att, v, preferred_element_type=f32))
            st_ref[h] = jnp.exp(b_last) * st + lax.dot_general(v, k_tail, tn, preferred_element_type=f32)
            on = o * lax.rsqrt(jnp.mean(o * o, axis=-1, keepdims=True) + EPS) * gn_ref[:, vs]
            r = r_ref[0, sl, vs]
            y_ref[0, sl, vs] = (on * (r * jax.nn.sigmoid(r))).astype(y_ref.dtype)
        return carry

    lax.fori_loop(0, n_chunks, body, 0)

    @pl.when(t == pl.num_programs(1) - 1)
    def _():
        for h in range(heads):
            s_ref[0, h] = st_ref[h].T


def _gla(proj3, g_low3, offs, dims, w_gate_up, b_gate, g_norm, s0_5, layer, name):
    b, length, _ = proj3.shape
    heads, dk, dv = dims
    q_off, k_off, v_off, r_off = offs
    rank = g_low3.shape[-1]
    chunk = _gcd(length, GLA_CHUNK)
    tl = _pick(length, 256, chunk)
    n_chunks = tl // chunk
    kdim, vdim = heads * dk, heads * dv
    assert q_off % kdim == 0 and k_off % kdim == 0 and v_off % vdim == 0 and r_off % vdim == 0
    qb, kb, vb, rb = q_off // kdim, k_off // kdim, v_off // vdim, r_off // vdim
    has_s0 = s0_5 is not None
    in_specs = [pl.BlockSpec((1, tl, kdim), lambda bi, t: (bi, t, qb)),
                pl.BlockSpec((1, tl, kdim), lambda bi, t: (bi, t, kb)),
                pl.BlockSpec((1, tl, vdim), lambda bi, t: (bi, t, vb)),
                pl.BlockSpec((1, tl, vdim), lambda bi, t: (bi, t, rb)),
                pl.BlockSpec((1, tl, rank), lambda bi, t: (bi, t, 0)),
                pl.BlockSpec((rank, kdim), lambda bi, t: (0, 0)),
                pl.BlockSpec((1, kdim), lambda bi, t: (0, 0)),
                pl.BlockSpec((1, vdim), lambda bi, t: (0, 0))]
    args = [proj3, proj3, proj3, proj3, g_low3, w_gate_up, b_gate.reshape(1, -1), g_norm.reshape(1, -1)]
    if has_s0:
        in_specs.append(pl.BlockSpec((None, 1, heads, dk, dv), lambda bi, t: (layer, bi, 0, 0, 0)))
        args.append(s0_5)
    est = (2 * (2 * _nbytes((tl, kdim), f32) + 2 * _nbytes((tl, vdim), f32) + _nbytes((tl, V7X_LANES), f32))
           + 2 * _nbytes((tl, vdim), bf16) + 7 * _nbytes((heads, dk, dv), f32) + 16 * _nbytes((chunk, vdim), f32))
    return pl.pallas_call(
        functools.partial(_gla_kernel, chunk=chunk, n_chunks=n_chunks, heads=heads, scale=dk ** -0.5,
                          has_s0=has_s0),
        grid=(b, length // tl),
        in_specs=in_specs,
        out_specs=[pl.BlockSpec((1, tl, vdim), lambda bi, t: (bi, t, 0)),
                   pl.BlockSpec((1, heads, dk, dv), lambda bi, t: (bi, 0, 0, 0))],
        out_shape=[jax.ShapeDtypeStruct((b, length, vdim), bf16),
                   jax.ShapeDtypeStruct((b, heads, dk, dv), f32)],
        scratch_shapes=[pltpu.VMEM((heads, dv, dk), f32)],
        compiler_params=_params(2, est),
        name=name,
    )(*args)


def _gcd(a, b):
    while b:
        a, b = b, a % b
    return a


def _logf_kernel(fl_ref, bf_ref, lf_ref, c_ref):
    lf = _log_sigmoid(fl_ref[0] + bf_ref[...])
    lf_ref[0] = lf
    c = lf
    row = lax.broadcasted_iota(jnp.int32, c.shape, 0)
    s = 1
    while s < c.shape[0]:
        c = c + jnp.where(row >= s, pltpu.roll(c, s, 0), 0.0)
        s *= 2
    c_ref[0] = c


def _logf_cumsum(fl3, b_f, name):
    b, length, h = fl3.shape
    est = 16 * _nbytes((length, V7X_LANES), f32)
    return pl.pallas_call(
        _logf_kernel,
        grid=(b,),
        in_specs=[pl.BlockSpec((1, length, h), lambda bi: (bi, 0, 0)),
                  pl.BlockSpec((1, h), lambda bi: (0, 0))],
        out_specs=[pl.BlockSpec((1, length, h), lambda bi: (bi, 0, 0)),
                   pl.BlockSpec((1, length, h), lambda bi: (bi, 0, 0))],
        out_shape=[jax.ShapeDtypeStruct((b, length, h), f32),
                   jax.ShapeDtypeStruct((b, length, h), f32)],
        compiler_params=_params(1, est),
        name=name,
    )(fl3, b_f.reshape(1, h))


def _flash_kernel(q_ref, k_ref, v_ref, c_ref, o_ref, crow_ref, ccol_ref, *, tq, n_heads, scale):
    h = pl.program_id(1)
    qi = pl.program_id(2)
    nt = (((1,), (1,)), ((), ()))
    dh = q_ref.shape[2]

    @pl.when(qi == 0)
    def _():
        sel_col = (lax.broadcasted_iota(jnp.int32, (n_heads, V7X_LANES), 0) == h).astype(f32)
        sel_row = (lax.broadcasted_iota(jnp.int32, (V7X_SUBLANES, n_heads), 1) == h).astype(f32)
        ccol_ref[...] = jnp.dot(c_ref[0], sel_col, precision=HIGHEST, preferred_element_type=f32)
        crow_ref[...] = lax.dot_general(sel_row, c_ref[0], nt, precision=HIGHEST, preferred_element_type=f32)

    q0 = pl.multiple_of(qi * tq, tq)
    c_q = jnp.tile(ccol_ref[pl.ds(q0, tq), :], (1, tq // V7X_LANES))
    q = q_ref[0] * scale

    def step(j, carry, masked):
        m, l, acc = carry
        k0 = pl.multiple_of(j * tq, tq)
        kb = k_ref[0, pl.ds(k0, tq), :]
        vb = v_ref[0, pl.ds(k0, tq), :]
        s = lax.dot_general(q, kb, nt, preferred_element_type=f32)
        s = s + (c_q - crow_ref[0:1, pl.ds(k0, tq)])
        if masked:
            ri = lax.broadcasted_iota(jnp.int32, (tq, tq), 0)
            ci = lax.broadcasted_iota(jnp.int32, (tq, tq), 1)
            s = jnp.where(ci <= ri, s, -jnp.inf)
        m_new = jnp.maximum(m, jnp.max(s, axis=-1, keepdims=True))
        p = jnp.exp(s - m_new)
        alpha = jnp.exp(m - m_new)
        l = alpha * l + jnp.sum(p, axis=-1, keepdims=True)
        acc = alpha * acc + jnp.dot(p, vb, preferred_element_type=f32)
        return m_new, l, acc

    init = (jnp.full((tq, 1), -jnp.inf, f32), jnp.zeros((tq, 1), f32), jnp.zeros((tq, dh), f32))
    carry = lax.fori_loop(0, qi, functools.partial(step, masked=False), init)
    _, l, acc = step(qi, carry, True)
    o_ref[0] = (acc / l).astype(o_ref.dtype)


def _flash(q3, k3, v3, c3, n_heads, name):
    b, length, d = q3.shape
    dh = d // n_heads
    tq = _pick(length, 512, V7X_LANES)
    est = (2 * _nbytes((tq, dh), f32) + 4 * _nbytes((length, dh), f32) + 2 * _nbytes((length, V7X_LANES), f32)
           + 2 * _nbytes((tq, dh), bf16) + _nbytes((V7X_SUBLANES, length), f32) + 8 * _nbytes((tq, tq), f32))
    return pl.pallas_call(
        functools.partial(_flash_kernel, tq=tq, n_heads=n_heads, scale=dh ** -0.5),
        grid=(b, n_heads, length // tq),
        in_specs=[pl.BlockSpec((1, tq, dh), lambda bi, h, qi: (bi, qi, h)),
                  pl.BlockSpec((1, length, dh), lambda bi, h, qi: (bi, 0, h)),
                  pl.BlockSpec((1, length, dh), lambda bi, h, qi: (bi, 0, h)),
                  pl.BlockSpec((1, length, n_heads), lambda bi, h, qi: (bi, 0, 0))],
        out_specs=pl.BlockSpec((1, tq, dh), lambda bi, h, qi: (bi, qi, h)),
        out_shape=jax.ShapeDtypeStruct((b, length, d), bf16),
        scratch_shapes=[pltpu.VMEM((V7X_SUBLANES, length), f32),
                        pltpu.VMEM((length, V7X_LANES), f32)],
        compiler_params=_params(3, est),
        name=name,
    )(q3, k3, v3, c3)


def _decode_kernel(pt_ref, q_ref, kn_ref, vn_ref, lfs_ref, kp_ref, vp_ref, fp_ref, o_ref,
                   qs_ref, m_ref, l_ref, acc_ref, carry_ref, *, scale):
    del pt_ref
    p = pl.program_id(1)
    lq, n_heads, dh = q_ref.shape[1], q_ref.shape[2], q_ref.shape[3]
    gh = V7X_SUBLANES
    n_groups = n_heads // gh
    nq = lq * gh
    nt = (((1,), (1,)), ((), ()))
    tn = (((0,), (0,)), ((), ()))
    s_i = lax.broadcasted_iota(jnp.int32, (gh, V7X_LANES), 0)
    c_i = lax.broadcasted_iota(jnp.int32, (gh, V7X_LANES), 1)
    diag = ((c_i % gh) == s_i) & (c_i < nq)

    def attend(k3, v3, f2, causal):
        nk = f2.shape[0]
        upper = (lax.broadcasted_iota(jnp.int32, (nk, nk), 1)
                 > lax.broadcasted_iota(jnp.int32, (nk, nk), 0)).astype(f32)
        beta = jnp.dot(upper, f2, precision=HIGHEST, preferred_element_type=f32) + carry_ref[...]
        carry_ref[...] = carry_ref[...] + jnp.sum(f2, axis=0, keepdims=True)
        keep = diag[None]
        if causal:
            key_i = lax.broadcasted_iota(jnp.int32, (nk, gh, V7X_LANES), 0)
            t_i = lax.broadcasted_iota(jnp.int32, (nk, gh, V7X_LANES), 2) // gh
            keep = keep & (key_i <= t_i)
        for a in range(n_groups):
            hs = slice(a * gh, (a + 1) * gh)
            ka = k3[:, hs, :].reshape(nk * gh, dh)
            va = v3[:, hs, :].reshape(nk * gh, dh)
            g = lax.dot_general(ka, qs_ref[a], nt, preferred_element_type=f32)
            pick = ((lax.broadcasted_iota(jnp.int32, (n_heads, V7X_LANES), 0)
                     == a * gh + lax.broadcasted_iota(jnp.int32, (n_heads, V7X_LANES), 1) % gh)
                    & (lax.broadcasted_iota(jnp.int32, (n_heads, V7X_LANES), 1) < nq)).astype(f32)
            xa = jnp.dot(beta, pick, precision=HIGHEST, preferred_element_type=f32)
            g3 = g.reshape(nk, gh, V7X_LANES) + xa[:, None, :]
            g3 = jnp.where(keep, g3, -jnp.inf)
            m_old = m_ref[a]
            m_new = jnp.maximum(m_old, jnp.max(g3, axis=(0, 1)).reshape(1, V7X_LANES))
            m_safe = jnp.where(m_new == -jnp.inf, 0.0, m_new)
            alpha = jnp.exp(m_old - m_safe)
            p3 = jnp.exp(g3 - m_safe[None])
            l_ref[a] = alpha * l_ref[a] + jnp.sum(p3, axis=(0, 1)).reshape(1, V7X_LANES)
            acc_ref[a] = alpha * acc_ref[a] + lax.dot_general(
                va, p3.reshape(nk * gh, V7X_LANES), tn, preferred_element_type=f32)
            m_ref[a] = m_new

    @pl.when(p == 0)
    def _():
        for a in range(n_groups):
            qa = q_ref[0, :, a * gh:(a + 1) * gh, :].reshape(nq, dh) * scale
            qs_ref[a] = jnp.concatenate([qa, jnp.zeros((V7X_LANES - nq, dh), f32)], axis=0)
        m_ref[...] = jnp.full(m_ref.shape, -jnp.inf, f32)
        l_ref[...] = jnp.zeros(l_ref.shape, f32)
        acc_ref[...] = jnp.zeros(acc_ref.shape, f32)
        carry_ref[...] = jnp.zeros(carry_ref.shape, f32)
        attend(kn_ref[0], vn_ref[0], lfs_ref[0], True)

    @pl.when(p > 0)
    def _():
        attend(kp_ref, vp_ref, fp_ref[...], False)

    @pl.when(p == pl.num_programs(1) - 1)
    def _():
        for a in range(n_groups):
            out_t = acc_ref[a] / l_ref[a]
            out = out_t.T
            for t in range(lq):
                o_ref[0, t, a * gh:(a + 1) * gh, :] = out[t * gh:(t + 1) * gh, :]


def _decode_attention(q4, kn4, vn4, lfs3, cache_k, cache_v, cache_f, page_table, layer, name):
    b, lq, n_heads, dh = q4.shape
    n_pages = page_table.shape[1]
    page = cache_k.shape[2]
    assert n_heads % V7X_SUBLANES == 0 and lq * V7X_SUBLANES <= V7X_LANES and dh == V7X_LANES
    n_groups = n_heads // V7X_SUBLANES
    last = n_pages - 1
    page_of = lambda bi, p, pt: pt[bi, last - jnp.maximum(p - 1, 0)]
    new_blk = lambda bi, p, pt: (bi, 0, 0, 0)
    grid_spec = pltpu.PrefetchScalarGridSpec(
        num_scalar_prefetch=1,
        grid=(b, n_pages + 1),
        in_specs=[pl.BlockSpec((1, lq, n_heads, dh), new_blk),
                  pl.BlockSpec((1, lq, n_heads, dh), new_blk),
                  pl.BlockSpec((1, lq, n_heads, dh), new_blk),
                  pl.BlockSpec((1, lq, n_heads), lambda bi, p, pt: (bi, 0, 0)),
                  pl.BlockSpec((None, None, page, n_heads, dh),
                               lambda bi, p, pt: (layer, page_of(bi, p, pt), 0, 0, 0)),
                  pl.BlockSpec((None, None, page, n_heads, dh),
                               lambda bi, p, pt: (layer, page_of(bi, p, pt), 0, 0, 0)),
                  pl.BlockSpec((None, None, page, n_heads),
                               lambda bi, p, pt: (layer, page_of(bi, p, pt), 0, 0))],
        out_specs=pl.BlockSpec((1, lq, n_heads, dh), new_blk),
        scratch_shapes=[pltpu.VMEM((n_groups, V7X_LANES, dh), f32),
                        pltpu.VMEM((n_groups, 1, V7X_LANES), f32),
                        pltpu.VMEM((n_groups, 1, V7X_LANES), f32),
                        pltpu.VMEM((n_groups, dh, V7X_LANES), f32),
                        pltpu.VMEM((1, n_heads), f32)],
    )
    est = (4 * _nbytes((page, n_heads, dh), f32) + 8 * _nbytes((lq, n_heads, dh), f32)
           + 8 * _nbytes((page * V7X_SUBLANES, V7X_LANES), f32) + 4 * _nbytes((n_groups, V7X_LANES, dh), f32))
    return pl.pallas_call(
        functools.partial(_decode_kernel, scale=dh ** -0.5),
        grid_spec=grid_spec,
        out_shape=jax.ShapeDtypeStruct((b, lq, n_heads, dh), f32),
        compiler_params=_params(2, est),
        name=name,
    )(page_table, q4, kn4, vn4, lfs3, cache_k, cache_v, cache_f)


def _even_mixer(h, b, length, e, tag, heads, w_in_even, w_conv_a, w_gate_up, b_gate, g_gla_norm, w_out_even,
                state_conv_a, state_gla):
    a_width = w_conv_a.shape[-1]
    kdim = w_gate_up.shape[-1]
    rank = w_gate_up.shape[1]
    vdim = g_gla_norm.shape[-1]
    dk, dv = kdim // heads, vdim // heads
    wide = 3 * a_width + 2 * kdim + 2 * vdim
    proj = _mm(h, w_in_even, e, 0, wide, tm_target=1024, tn_target=512, w_is_nk=True, name=f"in_even_{tag}")
    g_low = _mm_tail(h, w_in_even[e, wide:wide + rank, :], name=f"in_even_gate_{tag}")
    proj3 = proj.reshape(b, length, wide)
    y_a, conv_state = _conv_a(proj3, a_width, w_conv_a[e], state_conv_a, e, name=f"conv_a_{tag}")
    q_off = 3 * a_width
    offs = (q_off, q_off + kdim, q_off + 2 * kdim, q_off + 2 * kdim + vdim)
    y_b, s_new = _gla(proj3, g_low.reshape(b, length, rank), offs, (heads, dk, dv),
                      w_gate_up[e], b_gate[e], g_gla_norm[e], state_gla, e, name=f"gla_{tag}")
    y = jnp.concatenate([y_a, y_b], axis=-1).reshape(b * length, a_width + vdim)
    mix = _mm(y, w_out_even, e, 0, w_out_even.shape[-1], tm_target=1024, tn_target=512,
              name=f"out_even_{tag}")
    return mix, conv_state, s_new


def _fox_qkv(h, b, length, o, tag, w_in_odd, b_forget):
    d = h.shape[1]
    n_heads = b_forget.shape[-1]
    qkv = [_mm(h, w_in_odd, o, c * d, d, tm_target=1024, tn_target=512, w_is_nk=True,
               name=f"fox_{'qkv'[c]}_{tag}").reshape(b, length, d) for c in range(3)]
    fl = _mm_tail(h, w_in_odd[o, 3 * d:3 * d + n_heads, :], name=f"fox_f_{tag}")
    logf, csum = _logf_cumsum(fl.reshape(b, length, n_heads), b_forget[o], name=f"fox_logf_{tag}")
    return qkv, logf, csum


def _ffn(h, b, length, l, tag, w_up, w_ffn_conv, b_ffn_conv, w_down, state_ffn_conv):
    d_ff = w_down.shape[1]
    if state_ffn_conv is None:
        act, hist = _ffn_up_act(h, b, length, w_up, l, d_ff, w_ffn_conv[l], b_ffn_conv[l],
                                name=f"ffn_up_{tag}")
    else:
        u = _mm(h, w_up, l, 0, 2 * d_ff, tm_target=1024, tn_target=512, name=f"ffn_up_{tag}")
        act, hist = _ffn_act(u.reshape(b, length, 2 * d_ff), d_ff, w_ffn_conv[l], b_ffn_conv[l],
                             state_ffn_conv, l, name=f"ffn_act_{tag}")
        act = act.reshape(b * length, d_ff)
    down = _mm(act, w_down, l, 0, w_down.shape[-1], tm_target=512, tn_target=512, name=f"ffn_down_{tag}")
    return down, hist


def kernel(x_prompt, x_sample, state_conv_a, state_gla, cache_k, cache_v, cache_logf, state_ffn_conv, page_table, g_mix_pre, g_mix_post, g_ffn_pre, g_ffn_post, w_in_even, w_conv_a, w_gate_up, b_gate, g_gla_norm, w_out_even, w_in_odd, b_forget, w_out_odd, w_up, w_ffn_conv, b_ffn_conv, w_down):
    bp, lp, d = x_prompt.shape
    bs, ls, _ = x_sample.shape
    depth = g_mix_pre.shape[0]
    n_heads = cache_k.shape[3]
    gla_heads = state_gla.shape[2]

    xp = x_prompt.reshape(bp * lp, d)
    xs = x_sample.reshape(bs * ls, d)
    ca_p, ca_s, sg_p, sg_s = [], [], [], []
    kp_l, vp_l, lp_l, ks_l, vs_l, ls_l = [], [], [], [], [], []
    fc_p, fc_s = [], []
    w_in_even_nk = jnp.swapaxes(w_in_even, 1, 2)
    w_in_odd_nk = jnp.swapaxes(w_in_odd, 1, 2)
    even_w = (w_in_even_nk, w_conv_a, w_gate_up, b_gate, g_gla_norm, w_out_even)
    ffn_w = (w_up, w_ffn_conv, b_ffn_conv, _round_bf16(w_down, name="round_w_down"))
    hp = _rms_cast(xp, g_mix_pre[0], name="mix_pre_p0")
    hs = _rms_cast(xs, g_mix_pre[0], name="mix_pre_s0")
    for l in range(depth):
        if l % 2 == 0:
            e = l // 2
            mp, c_p, s_p = _even_mixer(hp, bp, lp, e, f"p{l}", gla_heads, *even_w, None, None)
            ms, c_s, s_s = _even_mixer(hs, bs, ls, e, f"s{l}", gla_heads, *even_w, state_conv_a, state_gla)
            ca_p.append(c_p)
            ca_s.append(c_s)
            sg_p.append(s_p)
            sg_s.append(s_s)
        else:
            o = l // 2
            dh = d // n_heads
            (qp, kp, vp), lfp, cp = _fox_qkv(hp, bp, lp, o, f"p{l}", w_in_odd_nk, b_forget)
            ap = _flash(qp, kp, vp, cp, n_heads, name=f"fox_attn_p{l}")
            mp = _mm(ap.reshape(bp * lp, d), w_out_odd, o, 0, d, tm_target=1024, tn_target=512,
                     name=f"fox_out_p{l}")
            qkv_s, lfs, _ = _fox_qkv(hs, bs, ls, o, f"s{l}", w_in_odd_nk, b_forget)
            qs, kn, vn = (a.reshape(bs, ls, n_heads, dh) for a in qkv_s)
            a_s = _decode_attention(qs, kn, vn, lfs, cache_k, cache_v, cache_logf, page_table, o,
                                    name=f"fox_attn_s{l}")
            ms = _mm(a_s.reshape(bs * ls, d).astype(bf16), w_out_odd, o, 0, d, tm_target=1024, tn_target=512,
                     name=f"fox_out_s{l}")
            kp_l.append(kp.reshape(bp, lp, n_heads, dh))
            vp_l.append(vp.reshape(bp, lp, n_heads, dh))
            lp_l.append(lfp)
            ks_l.append(kn)
            vs_l.append(vn)
            ls_l.append(lfs)
        xp, hp = _post_pre(mp, g_mix_post[l], xp, g_ffn_pre[l], name=f"mix_post_p{l}")
        xs, hs = _post_pre(ms, g_mix_post[l], xs, g_ffn_pre[l], name=f"mix_post_s{l}")
        fp, h_p = _ffn(hp, bp, lp, l, f"p{l}", *ffn_w, None)
        fs, h_s = _ffn(hs, bs, ls, l, f"s{l}", *ffn_w, state_ffn_conv)
        fc_p.append(h_p)
        fc_s.append(h_s)
        if l + 1 < depth:
            xp, hp = _post_pre(fp, g_ffn_post[l], xp, g_mix_pre[l + 1], name=f"ffn_post_p{l}")
            xs, hs = _post_pre(fs, g_ffn_post[l], xs, g_mix_pre[l + 1], name=f"ffn_post_s{l}")
        else:
            xp = _post_residual(fp, g_ffn_post[l], xp, name=f"ffn_post_p{l}")
            xs = _post_residual(fs, g_ffn_post[l], xs, name=f"ffn_post_s{l}")

    return (xp.reshape(bp, lp, d), xs.reshape(bs, ls, d),
            jnp.stack(ca_p), jnp.stack(ca_s), jnp.stack(sg_p), jnp.stack(sg_s),
            jnp.stack(kp_l), jnp.stack(vp_l), jnp.stack(lp_l),
            jnp.stack(ks_l), jnp.stack(vs_l), jnp.stack(ls_l),
            jnp.stack(fc_p), jnp.stack(fc_s))
```
